```python
import math
import jax, jax.numpy as jnp
from jax import lax
import numpy as np

D_MODEL = 4096
BATCH = 4
SEQ = 2048
DEPTH = 4
DEC_BATCH = 8
DEC_SEQ = 8
PAST_LEN = 8192
PAGE_SIZE = 128

N_EVEN = (DEPTH + 1) // 2
N_ODD = DEPTH // 2
DH = 128
D_ATT = D_MODEL // 2
H_A = D_ATT // (2 * DH)
ATTN_SCALE = DH ** -0.5
Q_BLOCK = 128
N_BUCKETS = 32
MAX_DISTANCE = 128
D_SGU = D_MODEL - D_ATT
SGU_GROUPS = 8
SGU_CG = D_SGU // SGU_GROUPS
CHUNK = 128
D_IN = 3 * D_ATT + 2 * D_SGU
D_CONV = D_MODEL
CONV_W = 31
PEER_HEADS = 8
N_KEYS = 128
N_EXPERTS = N_KEYS * N_KEYS
PEER_TOPK = 16
D_KEY = 256
PEER_BLOCK = 64
RMS_EPS = 1e-6
LN_EPS = 1e-5

kernel_name = 'hybrid_diffattn_sgu_conformer_peer_step'


def rmsnorm(x, g):
    xf = x.astype(jnp.float32)
    y = xf * lax.rsqrt(jnp.mean(xf * xf, axis=-1, keepdims=True) + RMS_EPS)
    return (y * g.astype(jnp.float32)).astype(x.dtype)


def layernorm(x, g, b):
    xf = x.astype(jnp.float32)
    mu = jnp.mean(xf, axis=-1, keepdims=True)
    xc = xf - mu
    var = jnp.mean(xc * xc, axis=-1, keepdims=True)
    return (xc * lax.rsqrt(var + LN_EPS) * g.astype(jnp.float32) + b.astype(jnp.float32)).astype(x.dtype)


def rel_bucket(dist):
    n = jnp.maximum(dist, 0)
    max_exact = N_BUCKETS // 2
    nf = jnp.maximum(n, 1).astype(jnp.float32)
    large = max_exact + (jnp.log(nf / max_exact) / math.log(MAX_DISTANCE / max_exact)
                         * (N_BUCKETS - max_exact)).astype(jnp.int32)
    large = jnp.minimum(large, N_BUCKETS - 1)
    return jnp.where(n < max_exact, n, large)


def diff_attend(q, k, v, q_pos, k_pos, lam, rel_bias):
    dist = q_pos[:, None] - k_pos[None, :]
    bias = jnp.moveaxis(rel_bias[rel_bucket(dist)], -1, 0).astype(jnp.float32)
    s = jnp.einsum('bqhcd,bkhcd->bhcqk', q, k).astype(jnp.float32) * ATTN_SCALE + bias[None, :, None]
    s = jnp.where(dist >= 0, s, -jnp.inf)
    p = jax.nn.softmax(s, axis=-1)
    w = p[:, :, 0] - lam * p[:, :, 1]
    return jnp.einsum('bhqk,bkhe->bqhe', w.astype(v.dtype), v)


def chunk_mix(g, w_s, b_s):
    B, T, G, C = g.shape
    pad = (-T) % CHUNK
    gc = jnp.pad(g, ((0, 0), (0, pad), (0, 0), (0, 0))).reshape(B, (T + pad) // CHUNK, CHUNK, G, C)
    w = jnp.where(jnp.tril(jnp.ones((CHUNK, CHUNK), dtype=bool)), w_s, 0.0)
    m = jnp.einsum('gts,bnsgc->bntgc', w.astype(g.dtype), gc) + b_s.T[None, None, :, :, None].astype(g.dtype)
    return m.reshape(B, T + pad, G, C)[:, :T]


def even_mixer(xm, i, layer_idx, P, past_k, past_v):
    B, T, _ = xm.shape
    f32 = jnp.float32
    proj = xm @ P['w_in'][i]
    q, k, v, u, g = jnp.split(proj, [D_ATT, 2 * D_ATT, 3 * D_ATT, 3 * D_ATT + D_SGU], axis=-1)
    q = rmsnorm(q.reshape(B, T, H_A, 2, DH), P['q_gain'][i])
    k = rmsnorm(k.reshape(B, T, H_A, 2, DH), P['k_gain'][i])
    v = v.reshape(B, T, H_A, 2 * DH)
    lam_init = 0.8 - 0.6 * math.exp(-0.3 * layer_idx)
    lam = (jnp.exp(jnp.sum(P['lam_q1'][i].astype(f32) * P['lam_k1'][i].astype(f32)))
           - jnp.exp(jnp.sum(P['lam_q2'][i].astype(f32) * P['lam_k2'][i].astype(f32))) + lam_init)
    if past_k is None:
        nb = T // Q_BLOCK
        qb = q.reshape(B, nb, Q_BLOCK, H_A, 2, DH).swapaxes(0, 1)
        starts = jnp.arange(nb, dtype=jnp.int32) * Q_BLOCK
        k_pos = jnp.arange(T, dtype=jnp.int32)

        def block(args):
            qblk, start = args
            return diff_attend(qblk, k, v, start + jnp.arange(Q_BLOCK, dtype=jnp.int32), k_pos, lam, P['rel_bias'])

        o = lax.map(block, (qb, starts)).swapaxes(0, 1).reshape(B, T, H_A, 2 * DH)
    else:
        past = past_k.shape[1]
        k_all = jnp.concatenate([past_k.astype(k.dtype), k], axis=1)
        v_all = jnp.concatenate([past_v.astype(v.dtype), v], axis=1)
        o = diff_attend(q, k_all, v_all, past + jnp.arange(T, dtype=jnp.int32),
                        jnp.arange(past + T, dtype=jnp.int32), lam, P['rel_bias'])
    attn = (rmsnorm(o, P['subln_gain'][i]) * (1.0 - lam_init)).reshape(B, T, D_ATT)
    u = jax.nn.gelu(u, approximate=False)
    g = layernorm(jax.nn.gelu(g, approximate=False), P['sgu_ln_gain'][i], P['sgu_ln_bias'][i])
    s_out = u * chunk_mix(g.reshape(B, T, SGU_GROUPS, SGU_CG), P['sgu_w'][i], P['sgu_b'][i]).reshape(B, T, D_SGU)
    y = jnp.concatenate([attn, s_out], axis=-1) @ P['w_out'][i]
    return y, k.reshape(B, T, H_A, 2 * DH), v, g


def conv_mixer(xm, i, P, buf):
    a = xm @ P['w_pw1'][i] + P['b_pw1'][i]
    val, gate = jnp.split(a, 2, axis=-1)
    glu = val * jax.nn.sigmoid(gate)
    full = jnp.concatenate([buf.astype(glu.dtype), glu], axis=1)
    h = lax.conv_general_dilated(full, P['w_dw'][i][:, None, :].astype(full.dtype), (1,), 'VALID',
                                 dimension_numbers=('NWC', 'WIO', 'NWC'),
                                 feature_group_count=D_CONV) + P['b_dw'][i]
    h = jax.nn.silu(layernorm(h, P['conv_ln_gain'][i], P['conv_ln_bias'][i]))
    return h @ P['w_pw2'][i] + P['b_pw2'][i], full[:, -(CONV_W - 1):]


def peer(xm, l, P):
    B, T, D = xm.shape
    n = B * T
    blk = min(PEER_BLOCK, n)
    pad = (-n) % blk
    xf = jnp.pad(xm.reshape(n, D), ((0, pad), (0, 0)))
    wq, sk, eu, ev = P['w_query'][l], P['sub_keys'][l], P['expert_u'][l], P['expert_v'][l]

    def one(xb):
        q = (xb @ wq).reshape(blk, PEER_HEADS, 2, D_KEY // 2)
        s = jnp.einsum('thcd,ckd->thck', q, sk).astype(jnp.float32)
        sv, si = lax.top_k(s, PEER_TOPK)
        cand = (sv[:, :, 0, :, None] + sv[:, :, 1, None, :]).reshape(blk, PEER_HEADS, PEER_TOPK * PEER_TOPK)
        cid = (si[:, :, 0, :, None] * N_KEYS + si[:, :, 1, None, :]).reshape(blk, PEER_HEADS, PEER_TOPK * PEER_TOPK)
        best, pos = lax.top_k(cand, PEER_TOPK)
        eid = jnp.take_along_axis(cid, pos, axis=-1)
        gate = jax.nn.softmax(best, axis=-1)
        h = jnp.einsum('td,thkd->thk', xb, eu[eid])
        a = (jax.nn.gelu(h.astype(jnp.float32), approximate=False) * gate).astype(xb.dtype)
        return jnp.einsum('thk,thkd->td', a, ev[eid])

    out = lax.map(one, xf.reshape(-1, blk, D))
    return out.reshape(-1, D)[:n].reshape(B, T, D)


def trunk(x, c, P, cache_k, cache_v, conv_state, page_table):
    B = x.shape[0]
    sample = page_table is not None
    new_k, new_v, new_g, new_conv = [], [], [], []
    cs = jax.nn.silu(c)
    for l in range(DEPTH):
        mod = (cs @ P['w_ada'][l] + P['b_ada'][l])[:, None, :]
        sh1, sc1, gt1, sh2, sc2, gt2 = jnp.split(mod, 6, axis=-1)
        xm = rmsnorm(x, P['norm_gain'][l, 0]) * (1 + sc1) + sh1
        i = l // 2
        if l % 2 == 0:
            if sample:
                past = page_table.shape[1] * PAGE_SIZE
                pk = cache_k[i][page_table].reshape(B, past, H_A, 2, DH)
                pv = cache_v[i][page_table].reshape(B, past, H_A, 2 * DH)
            else:
                pk, pv = None, None
            mix, kr, vr, gr = even_mixer(xm, i, l, P, pk, pv)
            new_k.append(kr)
            new_v.append(vr)
            if sample:
                new_g.append(gr)
        else:
            buf = conv_state[i] if sample else jnp.zeros((B, CONV_W - 1, D_CONV), x.dtype)
            mix, cb = conv_mixer(xm, i, P, buf)
            new_conv.append(cb)
        x = x + gt1 * mix
        hm = rmsnorm(x, P['norm_gain'][l, 1]) * (1 + sc2) + sh2
        x = x + gt2 * peer(hm, l, P)
    g_rows = jnp.stack(new_g) if sample else None
    return x, jnp.stack(new_k), jnp.stack(new_v), g_rows, jnp.stack(new_conv)


def setup_inputs(seed: int = 0) -> dict:
    key = jax.random.key(seed)
    ks = iter(jax.random.split(key, 48))
    f32 = jnp.float32

    def nrm(shape, scale=1.0):
        return jax.random.normal(next(ks), shape, f32) * scale

    n_pages = PAST_LEN // PAGE_SIZE
    n_used = DEC_BATCH * n_pages
    n_pool = n_used + max(1, n_used // 4)
    page_table = jax.random.permutation(next(ks), n_pool)[:n_used].reshape(DEC_BATCH, n_pages).astype(jnp.int32)
    return {
        'x_prompt': nrm((BATCH, SEQ, D_MODEL)),
        'x_sample': nrm((DEC_BATCH, DEC_SEQ, D_MODEL)),
        'c_prompt': nrm((BATCH, D_MODEL)),
        'c_sample': nrm((DEC_BATCH, D_MODEL)),
        'cache_k': nrm((N_EVEN, n_pool, PAGE_SIZE, H_A, 2 * DH)),
        'cache_v': nrm((N_EVEN, n_pool, PAGE_SIZE, H_A, 2 * DH)),
        'state_conv': nrm((N_ODD, DEC_BATCH, CONV_W - 1, D_CONV), 0.5),
        'page_table': page_table,
        'norm_gain': 1.0 + nrm((DEPTH, 2, D_MODEL), 0.02),
        'w_ada': nrm((DEPTH, D_MODEL, 6 * D_MODEL), 0.5 * D_MODEL ** -0.5),
        'b_ada': nrm((DEPTH, 6 * D_MODEL), 0.01),
        'rel_bias': nrm((N_BUCKETS, H_A), 0.5),
        'w_in': nrm((N_EVEN, D_MODEL, D_IN), D_MODEL ** -0.5),
        'q_gain': 1.0 + nrm((N_EVEN, DH), 0.02),
        'k_gain': 1.0 + nrm((N_EVEN, DH), 0.02),
        'lam_q1': nrm((N_EVEN, DH), 0.1),
        'lam_k1': nrm((N_EVEN, DH), 0.1),
        'lam_q2': nrm((N_EVEN, DH), 0.1),
        'lam_k2': nrm((N_EVEN, DH), 0.1),
        'subln_gain': 1.0 + nrm((N_EVEN, 2 * DH), 0.02),
        'sgu_ln_gain': 1.0 + nrm((N_EVEN, D_SGU), 0.02),
        'sgu_ln_bias': nrm((N_EVEN, D_SGU), 0.02),
        'sgu_w': nrm((N_EVEN, SGU_GROUPS, CHUNK, CHUNK), CHUNK ** -0.5),
        'sgu_b': 1.0 + nrm((N_EVEN, SGU_GROUPS, CHUNK), 0.02),
        'w_out': nrm((N_EVEN, D_ATT + D_SGU, D_MODEL), (D_ATT + D_SGU) ** -0.5),
        'w_pw1': nrm((N_ODD, D_MODEL, 2 * D_CONV), D_MODEL ** -0.5),
        'b_pw1': nrm((N_ODD, 2 * D_CONV), 0.02),
        'w_dw': nrm((N_ODD, CONV_W, D_CONV), CONV_W ** -0.5),
        'b_dw': nrm((N_ODD, D_CONV), 0.02),
        'conv_ln_gain': 1.0 + nrm((N_ODD, D_CONV), 0.02),
        'conv_ln_bias': nrm((N_ODD, D_CONV), 0.02),
        'w_pw2': nrm((N_ODD, D_CONV, D_MODEL), D_CONV ** -0.5),
        'b_pw2': nrm((N_ODD, D_MODEL), 0.02),
        'w_query': nrm((DEPTH, D_MODEL, PEER_HEADS * D_KEY), D_MODEL ** -0.5),
        'sub_keys': nrm((DEPTH, 2, N_KEYS, D_KEY // 2), (D_KEY // 2) ** -0.5),
        'expert_u': nrm((DEPTH, N_EXPERTS, D_MODEL), D_MODEL ** -0.5),
        'expert_v': nrm((DEPTH, N_EXPERTS, D_MODEL), PEER_HEADS ** -0.5),
    }


def reference(x_prompt, x_sample, c_prompt, c_sample, cache_k, cache_v, state_conv, page_table,
              norm_gain, w_ada, b_ada, rel_bias, w_in, q_gain, k_gain, lam_q1, lam_k1, lam_q2, lam_k2,
              subln_gain, sgu_ln_gain, sgu_ln_bias, sgu_w, sgu_b, w_out, w_pw1, b_pw1, w_dw, b_dw,
              conv_ln_gain, conv_ln_bias, w_pw2, b_pw2, w_query, sub_keys, expert_u, expert_v):
    P = dict(norm_gain=norm_gain, w_ada=w_ada, b_ada=b_ada, rel_bias=rel_bias, w_in=w_in,
             q_gain=q_gain, k_gain=k_gain, lam_q1=lam_q1, lam_k1=lam_k1, lam_q2=lam_q2, lam_k2=lam_k2,
             subln_gain=subln_gain, sgu_ln_gain=sgu_ln_gain, sgu_ln_bias=sgu_ln_bias, sgu_w=sgu_w,
             sgu_b=sgu_b, w_out=w_out, w_pw1=w_pw1, b_pw1=b_pw1, w_dw=w_dw, b_dw=b_dw,
             conv_ln_gain=conv_ln_gain, conv_ln_bias=conv_ln_bias, w_pw2=w_pw2, b_pw2=b_pw2,
             w_query=w_query, sub_keys=sub_keys, expert_u=expert_u, expert_v=expert_v)
    y_prompt, k_prompt, v_prompt, _, conv_prompt = trunk(x_prompt, c_prompt, P, None, None, None, None)
    y_sample, k_sample, v_sample, sgu_v_sample, conv_sample = trunk(
        x_sample, c_sample, P, cache_k, cache_v, state_conv, page_table)
    return (y_prompt, y_sample, k_prompt, v_prompt, k_sample, v_sample, sgu_v_sample, conv_prompt, conv_sample)
```

```python
import functools
import math

import jax
import jax.numpy as jnp
from jax import lax
from jax.experimental import pallas as pl
from jax.experimental.pallas import tpu as pltpu

D_MODEL = 4096
DEPTH = 4
PAGE_SIZE = 128
DH = 128
D_ATT = D_MODEL // 2
H_A = D_ATT // (2 * DH)
ATTN_SCALE = DH ** -0.5
Q_BLOCK = 128
N_BUCKETS = 32
MAX_DISTANCE = 128
D_SGU = D_MODEL - D_ATT
SGU_GROUPS = 8
SGU_CG = D_SGU // SGU_GROUPS
CHUNK = 128
D_CONV = D_MODEL
CONV_W = 31
PEER_HEADS = 8
N_KEYS = 128
PEER_TOPK = 16
D_KEY = 256
PEER_BLOCK = 64
RMS_EPS = 1e-6
LN_EPS = 1e-5

VMEM_LIMIT_BYTES = 48 * 1024 * 1024


def _mm_kernel(x_ref, w_ref, b_ref, o_ref, acc_ref):
    k = pl.program_id(2)

    @pl.when(k == 0)
    def _():
        acc_ref[...] = jnp.zeros_like(acc_ref)

    acc_ref[...] += jnp.dot(x_ref[...].astype(jnp.bfloat16), w_ref[...].astype(jnp.bfloat16),
                            preferred_element_type=jnp.float32)

    @pl.when(k == pl.num_programs(2) - 1)
    def _():
        o_ref[...] = acc_ref[...] + b_ref[...]


def _pick(dim, pref):
    for c in pref:
        if dim % c == 0:
            return c
    return dim


def mm(x, w, b=None):
    m, kdim = x.shape
    n = w.shape[1]
    mp = -(-m // 8) * 8
    if mp != m:
        x = jnp.pad(x, ((0, mp - m), (0, 0)))
    if b is None:
        b = jnp.zeros((n,), jnp.float32)
    bm = _pick(mp, (1024, 512, 256, 128, 64, 32, 16, 8))
    bn = _pick(n, (1024, 512, 256, 128))
    bk = _pick(kdim, (512, 256, 128))
    out = pl.pallas_call(
        _mm_kernel,
        grid=(mp // bm, n // bn, kdim // bk),
        in_specs=[pl.BlockSpec((bm, bk), lambda i, j, k: (i, k)),
                  pl.BlockSpec((bk, bn), lambda i, j, k: (k, j)),
                  pl.BlockSpec((1, bn), lambda i, j, k: (0, j))],
        out_specs=pl.BlockSpec((bm, bn), lambda i, j, k: (i, j)),
        out_shape=jax.ShapeDtypeStruct((mp, n), jnp.float32),
        scratch_shapes=[pltpu.VMEM((bm, bn), jnp.float32)],
        compiler_params=pltpu.CompilerParams(
            dimension_semantics=("parallel", "parallel", "arbitrary"),
            vmem_limit_bytes=VMEM_LIMIT_BYTES),
        name="mm",
    )(x, w, b.reshape(1, n))
    return out[:m]


def mm3(x, w, b=None):
    bsz, t, d = x.shape
    return mm(x.reshape(bsz * t, d), w, b).reshape(bsz, t, w.shape[1])


def rmsnorm(x, g):
    y = x * lax.rsqrt(jnp.mean(x * x, axis=-1, keepdims=True) + RMS_EPS)
    return y * g


def layernorm(x, g, b):
    mu = jnp.mean(x, axis=-1, keepdims=True)
    xc = x - mu
    var = jnp.mean(xc * xc, axis=-1, keepdims=True)
    return xc * lax.rsqrt(var + LN_EPS) * g + b


def rel_bucket(dist):
    n = jnp.maximum(dist, 0)
    max_exact = N_BUCKETS // 2
    nf = jnp.maximum(n, 1).astype(jnp.float32)
    large = max_exact + (jnp.log(nf / max_exact) / math.log(MAX_DISTANCE / max_exact)
                         * (N_BUCKETS - max_exact)).astype(jnp.int32)
    large = jnp.minimum(large, N_BUCKETS - 1)
    return jnp.where(n < max_exact, n, large)


def diff_attend(q, k, v, q_pos, k_pos, lam, rel_bias):
    dist = q_pos[:, None] - k_pos[None, :]
    bias = jnp.moveaxis(rel_bias[rel_bucket(dist)], -1, 0).astype(jnp.float32)
    s = jnp.einsum('bqhcd,bkhcd->bhcqk', q, k).astype(jnp.float32) * ATTN_SCALE + bias[None, :, None]
    s = jnp.where(dist >= 0, s, -jnp.inf)
    p = jax.nn.softmax(s, axis=-1)
    w = p[:, :, 0] - lam * p[:, :, 1]
    return jnp.einsum('bhqk,bkhe->bqhe', w.astype(v.dtype), v)


def chunk_mix(g, w_s, b_s):
    B, T, G, C = g.shape
    pad = (-T) % CHUNK
    gc = jnp.pad(g, ((0, 0), (0, pad), (0, 0), (0, 0))).reshape(B, (T + pad) // CHUNK, CHUNK, G, C)
    w = jnp.where(jnp.tril(jnp.ones((CHUNK, CHUNK), dtype=bool)), w_s, 0.0)
    m = jnp.einsum('gts,bnsgc->bntgc', w.astype(g.dtype), gc) + b_s.T[None, None, :, :, None].astype(g.dtype)
    return m.reshape(B, T + pad, G, C)[:, :T]


def even_mixer(xm, i, layer_idx, P, past_k, past_v):
    B, T, _ = xm.shape
    f32 = jnp.float32
    proj = mm3(xm, P['w_in'][i])
    q, k, v, u, g = jnp.split(proj, [D_ATT, 2 * D_ATT, 3 * D_ATT, 3 * D_ATT + D_SGU], axis=-1)
    q = rmsnorm(q.reshape(B, T, H_A, 2, DH), P['q_gain'][i])
    k = rmsnorm(k.reshape(B, T, H_A, 2, DH), P['k_gain'][i])
    v = v.reshape(B, T, H_A, 2 * DH)
    lam_init = 0.8 - 0.6 * math.exp(-0.3 * layer_idx)
    lam = (jnp.exp(jnp.sum(P['lam_q1'][i].astype(f32) * P['lam_k1'][i].astype(f32)))
           - jnp.exp(jnp.sum(P['lam_q2'][i].astype(f32) * P['lam_k2'][i].astype(f32))) + lam_init)
    if past_k is None:
        nb = T // Q_BLOCK
        qb = q.reshape(B, nb, Q_BLOCK, H_A, 2, DH).swapaxes(0, 1)
        starts = jnp.arange(nb, dtype=jnp.int32) * Q_BLOCK
        k_pos = jnp.arange(T, dtype=jnp.int32)

        def block(args):
            qblk, start = args
            return diff_attend(qblk, k, v, start + jnp.arange(Q_BLOCK, dtype=jnp.int32), k_pos, lam, P['rel_bias'])

        o = lax.map(block, (qb, starts)).swapaxes(0, 1).reshape(B, T, H_A, 2 * DH)
    else:
        past = past_k.shape[1]
        k_all = jnp.concatenate([past_k.astype(k.dtype), k], axis=1)
        v_all = jnp.concatenate([past_v.astype(v.dtype), v], axis=1)
        o = diff_attend(q, k_all, v_all, past + jnp.arange(T, dtype=jnp.int32),
                        jnp.arange(past + T, dtype=jnp.int32), lam, P['rel_bias'])
    attn = (rmsnorm(o, P['subln_gain'][i]) * (1.0 - lam_init)).reshape(B, T, D_ATT)
    u = jax.nn.gelu(u, approximate=False)
    g = layernorm(jax.nn.gelu(g, approximate=False), P['sgu_ln_gain'][i], P['sgu_ln_bias'][i])
    s_out = u * chunk_mix(g.reshape(B, T, SGU_GROUPS, SGU_CG), P['sgu_w'][i], P['sgu_b'][i]).reshape(B, T, D_SGU)
    y = mm3(jnp.concatenate([attn, s_out], axis=-1), P['w_out'][i])
    return y, k.reshape(B, T, H_A, 2 * DH), v, g


def conv_mixer(xm, i, P, buf):
    a = mm3(xm, P['w_pw1'][i], P['b_pw1'][i])
    val, gate = jnp.split(a, 2, axis=-1)
    glu = val * jax.nn.sigmoid(gate)
    full = jnp.concatenate([buf.astype(glu.dtype), glu], axis=1)
    h = lax.conv_general_dilated(full, P['w_dw'][i][:, None, :].astype(full.dtype), (1,), 'VALID',
                                 dimension_numbers=('NWC', 'WIO', 'NWC'),
                                 feature_group_count=D_CONV) + P['b_dw'][i]
    h = jax.nn.silu(layernorm(h, P['conv_ln_gain'][i], P['conv_ln_bias'][i]))
    return mm3(h, P['w_pw2'][i], P['b_pw2'][i]), full[:, -(CONV_W - 1):]


def peer(xm, l, P):
    B, T, D = xm.shape
    n = B * T
    blk = min(PEER_BLOCK, n)
    xf = xm.reshape(n, D)
    wq, sk, eu, ev = P['w_query'][l], P['sub_keys'][l], P['expert_u'][l], P['expert_v'][l]
    qall = mm(xf, wq)

    def one(args):
        xb, qb = args
        q = qb.reshape(blk, PEER_HEADS, 2, D_KEY // 2)
        s = jnp.einsum('thcd,ckd->thck', q, sk).astype(jnp.float32)
        sv, si = lax.top_k(s, PEER_TOPK)
        cand = (sv[:, :, 0, :, None] + sv[:, :, 1, None, :]).reshape(blk, PEER_HEADS, PEER_TOPK * PEER_TOPK)
        cid = (si[:, :, 0, :, None] * N_KEYS + si[:, :, 1, None, :]).reshape(blk, PEER_HEADS, PEER_TOPK * PEER_TOPK)
        best, pos = lax.top_k(cand, PEER_TOPK)
        eid = jnp.take_along_axis(cid, pos, axis=-1)
        gate = jax.nn.softmax(best, axis=-1)
        h = jnp.einsum('td,thkd->thk', xb, eu[eid])
        a = (jax.nn.gelu(h.astype(jnp.float32), approximate=False) * gate).astype(xb.dtype)
        return jnp.einsum('thk,thkd->td', a, ev[eid])

    out = lax.map(one, (xf.reshape(-1, blk, D), qall.reshape(-1, blk, PEER_HEADS * D_KEY)))
    return out.reshape(B, T, D)


def trunk(x, c, P, cache_k, cache_v, conv_state, page_table):
    B = x.shape[0]
    sample = page_table is not None
    new_k, new_v, new_g, new_conv = [], [], [], []
    cs = jax.nn.silu(c)
    for l in range(DEPTH):
        mod = mm(cs, P['w_ada'][l], P['b_ada'][l])[:, None, :]
        sh1, sc1, gt1, sh2, sc2, gt2 = jnp.split(mod, 6, axis=-1)
        xm = rmsnorm(x, P['norm_gain'][l, 0]) * (1 + sc1) + sh1
        i = l // 2
        if l % 2 == 0:
            if sample:
                past = page_table.shape[1] * PAGE_SIZE
                pk = cache_k[i][page_table].reshape(B, past, H_A, 2, DH)
                pv = cache_v[i][page_table].reshape(B, past, H_A, 2 * DH)
            else:
                pk, pv = None, None
            mix, kr, vr, gr = even_mixer(xm, i, l, P, pk, pv)
            new_k.append(kr)
            new_v.append(vr)
            if sample:
                new_g.append(gr)
        else:
            buf = conv_state[i] if sample else jnp.zeros((B, CONV_W - 1, D_CONV), x.dtype)
            mix, cb = conv_mixer(xm, i, P, buf)
            new_conv.append(cb)
        x = x + gt1 * mix
        hm = rmsnorm(x, P['norm_gain'][l, 1]) * (1 + sc2) + sh2
        x = x + gt2 * peer(hm, l, P)
    g_rows = jnp.stack(new_g) if sample else None
    return x, jnp.stack(new_k), jnp.stack(new_v), g_rows, jnp.stack(new_conv)


def kernel(x_prompt, x_sample, c_prompt, c_sample, cache_k, cache_v, state_conv, page_table, norm_gain, w_ada, b_ada, rel_bias, w_in, q_gain, k_gain, lam_q1, lam_k1, lam_q2, lam_k2, subln_gain, sgu_ln_gain, sgu_ln_bias, sgu_w, sgu_b, w_out, w_pw1, b_pw1, w_dw, b_dw, conv_ln_gain, conv_ln_bias, w_pw2, b_pw2, w_query, sub_keys, expert_u, expert_v):
    P = dict(norm_gain=norm_gain, w_ada=w_ada, b_ada=b_ada, rel_bias=rel_bias, w_in=w_in,
             q_gain=q_gain, k_gain=k_gain, lam_q1=lam_q1, lam_k1=lam_k1, lam_q2=lam_q2, lam_k2=lam_k2,
             subln_gain=subln_gain, sgu_ln_gain=sgu_ln_gain, sgu_ln_bias=sgu_ln_bias, sgu_w=sgu_w,
             sgu_b=sgu_b, w_out=w_out, w_pw1=w_pw1, b_pw1=b_pw1, w_dw=w_dw, b_dw=b_dw,
             conv_ln_gain=conv_ln_gain, conv_ln_bias=conv_ln_bias, w_pw2=w_pw2, b_pw2=b_pw2,
             w_query=w_query, sub_keys=sub_keys, expert_u=expert_u, expert_v=expert_v)
    y_prompt, k_prompt, v_prompt, _, conv_prompt = trunk(x_prompt, c_prompt, P, None, None, None, None)
    y_sample, k_sample, v_sample, sgu_v_sample, conv_sample = trunk(
        x_sample, c_sample, P, cache_k, cache_v, state_conv, page_table)
    return (y_prompt, y_sample, k_prompt, v_prompt, k_sample, v_sample, sgu_v_sample, conv_prompt, conv_sample)
```

```python
import functools
import math

import jax
import jax.numpy as jnp
from jax import lax
from jax.experimental import pallas as pl
from jax.experimental.pallas import tpu as pltpu

D_MODEL = 4096
DEPTH = 4
PAGE_SIZE = 128
DH = 128
D_ATT = D_MODEL // 2
H_A = D_ATT // (2 * DH)
ATTN_SCALE = DH ** -0.5
Q_BLOCK = 128
N_BUCKETS = 32
MAX_DISTANCE = 128
D_SGU = D_MODEL - D_ATT
SGU_GROUPS = 8
SGU_CG = D_SGU // SGU_GROUPS
CHUNK = 128
D_CONV = D_MODEL
CONV_W = 31
PEER_HEADS = 8
N_KEYS = 128
PEER_TOPK = 16
D_KEY = 256
PEER_BLOCK = 64
RMS_EPS = 1e-6
LN_EPS = 1e-5

VMEM_LIMIT_BYTES = 48 * 1024 * 1024


def _mm_kernel(x_ref, w_ref, b_ref, o_ref, acc_ref):
    k = pl.program_id(2)

    @pl.when(k == 0)
    def _():
        acc_ref[...] = jnp.zeros_like(acc_ref)

    acc_ref[...] += jnp.dot(x_ref[...].astype(jnp.bfloat16), w_ref[...].astype(jnp.bfloat16),
                            preferred_element_type=jnp.float32)

    @pl.when(k == pl.num_programs(2) - 1)
    def _():
        o_ref[...] = acc_ref[...] + b_ref[...]


def _pick(dim, pref):
    for c in pref:
        if dim % c == 0:
            return c
    return dim


def mm(x, w, b=None):
    m, kdim = x.shape
    n = w.shape[1]
    mp = -(-m // 8) * 8
    if mp != m:
        x = jnp.pad(x, ((0, mp - m), (0, 0)))
    if b is None:
        b = jnp.zeros((n,), jnp.float32)
    bm = _pick(mp, (1024, 512, 256, 128, 64, 32, 16, 8))
    bn = _pick(n, (1024, 512, 256, 128))
    bk = _pick(kdim, (512, 256, 128))
    out = pl.pallas_call(
        _mm_kernel,
        grid=(mp // bm, n // bn, kdim // bk),
        in_specs=[pl.BlockSpec((bm, bk), lambda i, j, k: (i, k)),
                  pl.BlockSpec((bk, bn), lambda i, j, k: (k, j)),
                  pl.BlockSpec((1, bn), lambda i, j, k: (0, j))],
        out_specs=pl.BlockSpec((bm, bn), lambda i, j, k: (i, j)),
        out_shape=jax.ShapeDtypeStruct((mp, n), jnp.float32),
        scratch_shapes=[pltpu.VMEM((bm, bn), jnp.float32)],
        compiler_params=pltpu.CompilerParams(
            dimension_semantics=("parallel", "parallel", "arbitrary"),
            vmem_limit_bytes=VMEM_LIMIT_BYTES),
        name="mm",
    )(x, w, b.reshape(1, n))
    return out[:m]


def mm3(x, w, b=None):
    bsz, t, d = x.shape
    return mm(x.reshape(bsz * t, d), w, b).reshape(bsz, t, w.shape[1])


def _top16_cols(s, n_rows):
    rows = lax.broadcasted_iota(jnp.int32, s.shape, 0).astype(jnp.float32)
    vals, ids = [], []
    for _ in range(PEER_TOPK):
        m = jnp.max(s, axis=0, keepdims=True)
        idx = jnp.min(jnp.where(s == m, rows, float(n_rows)), axis=0, keepdims=True)
        s = jnp.where(rows == idx, -jnp.inf, s)
        vals.append(m)
        ids.append(idx)
    return jnp.concatenate(vals, axis=0), jnp.concatenate(ids, axis=0)


def _route_kernel(q_ref, sk_ref, eid_ref, gate_ref):
    half = D_KEY // 2
    nt = (((1,), (1,)), ((), ()))
    sv, si = [], []
    for c in range(2):
        qc = q_ref[:, c * half:(c + 1) * half].astype(jnp.bfloat16)
        s = lax.dot_general(sk_ref[c].astype(jnp.bfloat16), qc, nt,
                            preferred_element_type=jnp.float32)
        v, i = _top16_cols(s, N_KEYS)
        sv.append(v)
        si.append(i)
    cand = jnp.concatenate([sv[0][i:i + 1, :] + sv[1] for i in range(PEER_TOPK)], axis=0)
    cid = jnp.concatenate([si[0][i:i + 1, :] * float(N_KEYS) + si[1] for i in range(PEER_TOPK)], axis=0)
    rows = lax.broadcasted_iota(jnp.int32, cand.shape, 0).astype(jnp.float32)
    n_cand = PEER_TOPK * PEER_TOPK
    best, eids = [], []
    for _ in range(PEER_TOPK):
        m = jnp.max(cand, axis=0, keepdims=True)
        pos = jnp.min(jnp.where(cand == m, rows, float(n_cand)), axis=0, keepdims=True)
        hit = rows == pos
        eids.append(jnp.sum(jnp.where(hit, cid, 0.0), axis=0, keepdims=True))
        cand = jnp.where(hit, -jnp.inf, cand)
        best.append(m)
    best = jnp.concatenate(best, axis=0)
    e = jnp.exp(best - best[0:1, :])
    gate_ref[0] = e / jnp.sum(e, axis=0, keepdims=True)
    eid_ref[0] = jnp.concatenate(eids, axis=0).astype(jnp.int32)


def peer_route(q, sk):
    n = q.shape[0]
    tb = min(128, n)
    eid, gate = pl.pallas_call(
        _route_kernel,
        grid=(n // tb, PEER_HEADS),
        in_specs=[pl.BlockSpec((tb, D_KEY), lambda i, h: (i, h)),
                  pl.BlockSpec((2, N_KEYS, D_KEY // 2), lambda i, h: (0, 0, 0))],
        out_specs=[pl.BlockSpec((1, PEER_TOPK, tb), lambda i, h: (h, 0, i)),
                   pl.BlockSpec((1, PEER_TOPK, tb), lambda i, h: (h, 0, i))],
        out_shape=[jax.ShapeDtypeStruct((PEER_HEADS, PEER_TOPK, n), jnp.int32),
                   jax.ShapeDtypeStruct((PEER_HEADS, PEER_TOPK, n), jnp.float32)],
        compiler_params=pltpu.CompilerParams(dimension_semantics=("parallel", "parallel")),
        name="peer_route",
    )(q, sk)
    n_sel = PEER_HEADS * PEER_TOPK
    return eid.reshape(n_sel, n).T, gate.reshape(n_sel, n).T


PEER_GROUP = 8
N_SEL = PEER_HEADS * PEER_TOPK
SUBLANES = 8
LANES = 128


def _gelu_exact(x):
    return 0.5 * x * (1.0 + lax.erf(x * (2.0 ** -0.5)))


def _peer_kernel(eid_cur, eid_nxt, x_ref, gate_ref, eu_hbm, ev_hbm, out_ref, ubuf, vbuf, acol, sem):
    g = pl.program_id(0)
    n_groups = pl.num_programs(0)
    n_tiles = N_SEL // SUBLANES
    n_lane_tiles = D_MODEL // LANES

    def issue(eid_ref, table, buf, s):
        def body(t, carry):
            for j in range(N_SEL):
                row = eid_ref[0, t, j]
                pltpu.make_async_copy(table.at[pl.ds(row, 1), :], buf.at[t, pl.ds(j, 1), :], sem.at[s]).start()
            return carry
        lax.fori_loop(0, PEER_GROUP, body, 0)

    def wait(buf, s):
        pltpu.make_async_copy(buf, buf, sem.at[s]).wait()

    @pl.when(g == 0)
    def _():
        issue(eid_cur, eu_hbm, ubuf, 0)
        issue(eid_cur, ev_hbm, vbuf, 1)

    sub_iota = lax.broadcasted_iota(jnp.int32, (SUBLANES, LANES), 0)
    lane_iota = lax.broadcasted_iota(jnp.int32, (SUBLANES, LANES), 1)

    wait(ubuf, 0)

    for t in range(PEER_GROUP):
        xb = jnp.broadcast_to(x_ref[t:t + 1, :], (SUBLANES, D_MODEL))
        gb = jnp.broadcast_to(gate_ref[t:t + 1, :], (SUBLANES, N_SEL))
        for r in range(n_tiles):
            prod = ubuf[t, r * SUBLANES:(r + 1) * SUBLANES, :] * xb
            acc = prod[:, 0:LANES]
            for c in range(1, n_lane_tiles):
                acc = acc + prod[:, c * LANES:(c + 1) * LANES]
            h = jnp.sum(acc, axis=1, keepdims=True)
            gcol = jnp.sum(jnp.where(lane_iota == sub_iota + r * SUBLANES, gb, 0.0), axis=1, keepdims=True)
            acol[t, r * SUBLANES:(r + 1) * SUBLANES, :] = jnp.broadcast_to(_gelu_exact(h) * gcol, (SUBLANES, LANES))

    @pl.when(g + 1 < n_groups)
    def _():
        issue(eid_nxt, eu_hbm, ubuf, 0)

    wait(vbuf, 1)

    rows = []
    for t in range(PEER_GROUP):
        acc = jnp.zeros((SUBLANES, D_MODEL), jnp.float32)
        for r in range(n_tiles):
            a = acol[t, r * SUBLANES:(r + 1) * SUBLANES, :]
            acc = acc + vbuf[t, r * SUBLANES:(r + 1) * SUBLANES, :] * jnp.concatenate([a] * n_lane_tiles, axis=1)
        rows.append(jnp.sum(acc, axis=0, keepdims=True))
    out_ref[...] = jnp.concatenate(rows, axis=0)

    @pl.when(g + 1 < n_groups)
    def _():
        issue(eid_nxt, ev_hbm, vbuf, 1)


def peer_experts(x, eid, gate, eu, ev):
    n = x.shape[0]
    n_groups = n // PEER_GROUP
    eid3 = eid.reshape(n_groups, PEER_GROUP, N_SEL)
    smem_block = functools.partial(pl.BlockSpec, (1, PEER_GROUP, N_SEL), memory_space=pltpu.SMEM)
    buf = pltpu.VMEM((PEER_GROUP, N_SEL, D_MODEL), jnp.float32)
    return pl.pallas_call(
        _peer_kernel,
        grid=(n_groups,),
        in_specs=[smem_block(lambda g: (g, 0, 0)),
                  smem_block(lambda g: (jnp.minimum(g + 1, n_groups - 1), 0, 0)),
                  pl.BlockSpec((PEER_GROUP, D_MODEL), lambda g: (g, 0)),
                  pl.BlockSpec((PEER_GROUP, N_SEL), lambda g: (g, 0)),
                  pl.BlockSpec(memory_space=pl.ANY),
                  pl.BlockSpec(memory_space=pl.ANY)],
        out_specs=pl.BlockSpec((PEER_GROUP, D_MODEL), lambda g: (g, 0)),
        out_shape=jax.ShapeDtypeStruct((n, D_MODEL), jnp.float32),
        scratch_shapes=[buf, buf,
                        pltpu.VMEM((PEER_GROUP, N_SEL, LANES), jnp.float32),
                        pltpu.SemaphoreType.DMA((2,))],
        compiler_params=pltpu.CompilerParams(dimension_semantics=("arbitrary",),
                                             vmem_limit_bytes=VMEM_LIMIT_BYTES),
        name="peer_experts",
    )(eid3, eid3, x, gate, eu, ev)


ATT_BLOCK = 256
FAR_BUCKET_MIN_DIST = MAX_DISTANCE


def _attn_kernel(lam_ref, bfar_ref, q_ref, k_ref, v_ref, bd_ref, bs_ref, g_ref, o_ref, m_ref, l_ref, acc_ref,
                 *, out_scale):
    h = pl.program_id(1)
    i = pl.program_id(2)
    nt = (((1,), (1,)), ((), ()))
    m_ref[...] = jnp.full(m_ref.shape, -jnp.inf, jnp.float32)
    l_ref[...] = jnp.zeros(l_ref.shape, jnp.float32)
    acc_ref[...] = jnp.zeros(acc_ref.shape, jnp.float32)

    def step(start, bias):
        kb = k_ref[pl.ds(start, ATT_BLOCK), :]
        vb = v_ref[pl.ds(start, ATT_BLOCK), :]
        for c in range(2):
            s = lax.dot_general(q_ref[:, c * DH:(c + 1) * DH], kb[:, c * DH:(c + 1) * DH], nt,
                                preferred_element_type=jnp.float32) * ATTN_SCALE + bias
            m_old = m_ref[c]
            m_new = jnp.maximum(m_old, jnp.max(s, axis=1, keepdims=True))
            alpha = jnp.exp(m_old - m_new)
            p = jnp.exp(s - m_new[:, 0:1])
            l_ref[c] = alpha * l_ref[c] + jnp.sum(p, axis=1, keepdims=True)
            acc_ref[c] = acc_ref[c] * alpha[:, 0:1] + jnp.dot(p.astype(jnp.bfloat16), vb,
                                                             preferred_element_type=jnp.float32)
            m_ref[c] = m_new

    def far_body(j, carry):
        step(pl.multiple_of(j * ATT_BLOCK, ATT_BLOCK), bfar_ref[h])
        return carry

    lax.fori_loop(0, jnp.maximum(i - 1, 0), far_body, 0)

    @pl.when(i >= 1)
    def _():
        step(pl.multiple_of((i - 1) * ATT_BLOCK, ATT_BLOCK), bs_ref[0])

    step(pl.multiple_of(i * ATT_BLOCK, ATT_BLOCK), bd_ref[0])

    o = acc_ref[0] / l_ref[0][:, 0:1] - lam_ref[0] * (acc_ref[1] / l_ref[1][:, 0:1])
    y = o * lax.rsqrt(jnp.mean(o * o, axis=1, keepdims=True) + RMS_EPS)
    o_ref[...] = y * g_ref[...] * out_scale


def prompt_attention(q, k, v, lam, rel_bias, subln_gain, out_scale, seq):
    n, width = q.shape
    n_heads = width // (2 * DH)
    assert ATT_BLOCK >= FAR_BUCKET_MIN_DIST and seq % ATT_BLOCK == 0
    r = jnp.arange(ATT_BLOCK, dtype=jnp.int32)
    d_diag = r[:, None] - r[None, :]
    bias_diag = jnp.where(d_diag >= 0, jnp.moveaxis(rel_bias[rel_bucket(d_diag)], -1, 0), -jnp.inf)
    bias_sub = jnp.moveaxis(rel_bias[rel_bucket(d_diag + ATT_BLOCK)], -1, 0)
    bias_far = rel_bias[N_BUCKETS - 1]
    nq = seq // ATT_BLOCK
    smem = pl.BlockSpec(memory_space=pltpu.SMEM)
    return pl.pallas_call(
        functools.partial(_attn_kernel, out_scale=out_scale),
        grid=(n // seq, n_heads, nq),
        in_specs=[smem, smem,
                  pl.BlockSpec((ATT_BLOCK, 2 * DH), lambda b, h, i: (b * nq + i, h)),
                  pl.BlockSpec((seq, 2 * DH), lambda b, h, i: (b, h)),
                  pl.BlockSpec((seq, 2 * DH), lambda b, h, i: (b, h)),
                  pl.BlockSpec((1, ATT_BLOCK, ATT_BLOCK), lambda b, h, i: (h, 0, 0)),
                  pl.BlockSpec((1, ATT_BLOCK, ATT_BLOCK), lambda b, h, i: (h, 0, 0)),
                  pl.BlockSpec((1, 2 * DH), lambda b, h, i: (0, 0))],
        out_specs=pl.BlockSpec((ATT_BLOCK, 2 * DH), lambda b, h, i: (b * nq + i, h)),
        out_shape=jax.ShapeDtypeStruct((n, width), jnp.float32),
        scratch_shapes=[pltpu.VMEM((2, ATT_BLOCK, LANES), jnp.float32),
                        pltpu.VMEM((2, ATT_BLOCK, LANES), jnp.float32),
                        pltpu.VMEM((2, ATT_BLOCK, 2 * DH), jnp.float32)],
        compiler_params=pltpu.CompilerParams(dimension_semantics=("parallel", "parallel", "parallel"),
                                             vmem_limit_bytes=VMEM_LIMIT_BYTES),
        name="prompt_attention",
    )(lam.reshape(1), bias_far, q, k, v, bias_diag, bias_sub, subln_gain.reshape(1, 2 * DH))


def rmsnorm(x, g):
    y = x * lax.rsqrt(jnp.mean(x * x, axis=-1, keepdims=True) + RMS_EPS)
    return y * g


def layernorm(x, g, b):
    mu = jnp.mean(x, axis=-1, keepdims=True)
    xc = x - mu
    var = jnp.mean(xc * xc, axis=-1, keepdims=True)
    return xc * lax.rsqrt(var + LN_EPS) * g + b


def rel_bucket(dist):
    n = jnp.maximum(dist, 0)
    max_exact = N_BUCKETS // 2
    nf = jnp.maximum(n, 1).astype(jnp.float32)
    large = max_exact + (jnp.log(nf / max_exact) / math.log(MAX_DISTANCE / max_exact)
                         * (N_BUCKETS - max_exact)).astype(jnp.int32)
    large = jnp.minimum(large, N_BUCKETS - 1)
    return jnp.where(n < max_exact, n, large)


def diff_attend(q, k, v, q_pos, k_pos, lam, rel_bias):
    dist = q_pos[:, None] - k_pos[None, :]
    bias = jnp.moveaxis(rel_bias[rel_bucket(dist)], -1, 0).astype(jnp.float32)
    s = jnp.einsum('bqhcd,bkhcd->bhcqk', q, k).astype(jnp.float32) * ATTN_SCALE + bias[None, :, None]
    s = jnp.where(dist >= 0, s, -jnp.inf)
    p = jax.nn.softmax(s, axis=-1)
    w = p[:, :, 0] - lam * p[:, :, 1]
    return jnp.einsum('bhqk,bkhe->bqhe', w.astype(v.dtype), v)


def chunk_mix(g, w_s, b_s):
    B, T, G, C = g.shape
    pad = (-T) % CHUNK
    gc = jnp.pad(g, ((0, 0), (0, pad), (0, 0), (0, 0))).reshape(B, (T + pad) // CHUNK, CHUNK, G, C)
    w = jnp.where(jnp.tril(jnp.ones((CHUNK, CHUNK), dtype=bool)), w_s, 0.0)
    m = jnp.einsum('gts,bnsgc->bntgc', w.astype(g.dtype), gc) + b_s.T[None, None, :, :, None].astype(g.dtype)
    return m.reshape(B, T + pad, G, C)[:, :T]


def even_mixer(xm, i, layer_idx, P, past_k, past_v):
    B, T, _ = xm.shape
    f32 = jnp.float32
    proj = mm3(xm, P['w_in'][i])
    q, k, v, u, g = jnp.split(proj, [D_ATT, 2 * D_ATT, 3 * D_ATT, 3 * D_ATT + D_SGU], axis=-1)
    q = rmsnorm(q.reshape(B, T, H_A, 2, DH), P['q_gain'][i])
    k = rmsnorm(k.reshape(B, T, H_A, 2, DH), P['k_gain'][i])
    v = v.reshape(B, T, H_A, 2 * DH)
    lam_init = 0.8 - 0.6 * math.exp(-0.3 * layer_idx)
    lam = (jnp.exp(jnp.sum(P['lam_q1'][i].astype(f32) * P['lam_k1'][i].astype(f32)))
           - jnp.exp(jnp.sum(P['lam_q2'][i].astype(f32) * P['lam_k2'][i].astype(f32))) + lam_init)
    if past_k is None:
        bf16 = jnp.bfloat16
        attn = prompt_attention(q.reshape(B * T, D_ATT).astype(bf16), k.reshape(B * T, D_ATT).astype(bf16),
                                v.reshape(B * T, D_ATT).astype(bf16), lam, P['rel_bias'], P['subln_gain'][i],
                                1.0 - lam_init, T).reshape(B, T, D_ATT)
    else:
        past = past_k.shape[1]
        k_all = jnp.concatenate([past_k.astype(k.dtype), k], axis=1)
        v_all = jnp.concatenate([past_v.astype(v.dtype), v], axis=1)
        o = diff_attend(q, k_all, v_all, past + jnp.arange(T, dtype=jnp.int32),
                        jnp.arange(past + T, dtype=jnp.int32), lam, P['rel_bias'])
        attn = (rmsnorm(o, P['subln_gain'][i]) * (1.0 - lam_init)).reshape(B, T, D_ATT)
    u = jax.nn.gelu(u, approximate=False)
    g = layernorm(jax.nn.gelu(g, approximate=False), P['sgu_ln_gain'][i], P['sgu_ln_bias'][i])
    s_out = u * chunk_mix(g.reshape(B, T, SGU_GROUPS, SGU_CG), P['sgu_w'][i], P['sgu_b'][i]).reshape(B, T, D_SGU)
    y = mm3(jnp.concatenate([attn, s_out], axis=-1), P['w_out'][i])
    return y, k.reshape(B, T, H_A, 2 * DH), v, g


def conv_mixer(xm, i, P, buf):
    a = mm3(xm, P['w_pw1'][i], P['b_pw1'][i])
    val, gate = jnp.split(a, 2, axis=-1)
    glu = val * jax.nn.sigmoid(gate)
    full = jnp.concatenate([buf.astype(glu.dtype), glu], axis=1)
    h = lax.conv_general_dilated(full, P['w_dw'][i][:, None, :].astype(full.dtype), (1,), 'VALID',
                                 dimension_numbers=('NWC', 'WIO', 'NWC'),
                                 feature_group_count=D_CONV) + P['b_dw'][i]
    h = jax.nn.silu(layernorm(h, P['conv_ln_gain'][i], P['conv_ln_bias'][i]))
    return mm3(h, P['w_pw2'][i], P['b_pw2'][i]), full[:, -(CONV_W - 1):]


def peer(xm, l, P):
    B, T, D = xm.shape
    n = B * T
    xf = xm.reshape(n, D)
    eid, gate = peer_route(mm(xf, P['w_query'][l]), P['sub_keys'][l])
    return peer_experts(xf, eid, gate, P['expert_u'][l], P['expert_v'][l]).reshape(B, T, D)


def trunk(x, c, P, cache_k, cache_v, conv_state, page_table):
    B = x.shape[0]
    sample = page_table is not None
    new_k, new_v, new_g, new_conv = [], [], [], []
    cs = jax.nn.silu(c)
    for l in range(DEPTH):
        mod = mm(cs, P['w_ada'][l], P['b_ada'][l])[:, None, :]
        sh1, sc1, gt1, sh2, sc2, gt2 = jnp.split(mod, 6, axis=-1)
        xm = rmsnorm(x, P['norm_gain'][l, 0]) * (1 + sc1) + sh1
        i = l // 2
        if l % 2 == 0:
            if sample:
                past = page_table.shape[1] * PAGE_SIZE
                pk = cache_k[i][page_table].reshape(B, past, H_A, 2, DH)
                pv = cache_v[i][page_table].reshape(B, past, H_A, 2 * DH)
            else:
                pk, pv = None, None
            mix, kr, vr, gr = even_mixer(xm, i, l, P, pk, pv)
            new_k.append(kr)
            new_v.append(vr)
            if sample:
                new_g.append(gr)
        else:
            buf = conv_state[i] if sample else jnp.zeros((B, CONV_W - 1, D_CONV), x.dtype)
            mix, cb = conv_mixer(xm, i, P, buf)
            new_conv.append(cb)
        x = x + gt1 * mix
        hm = rmsnorm(x, P['norm_gain'][l, 1]) * (1 + sc2) + sh2
        x = x + gt2 * peer(hm, l, P)
    g_rows = jnp.stack(new_g) if sample else None
    return x, jnp.stack(new_k), jnp.stack(new_v), g_rows, jnp.stack(new_conv)


def kernel(x_prompt, x_sample, c_prompt, c_sample, cache_k, cache_v, state_conv, page_table, norm_gain, w_ada, b_ada, rel_bias, w_in, q_gain, k_gain, lam_q1, lam_k1, lam_q2, lam_k2, subln_gain, sgu_ln_gain, sgu_ln_bias, sgu_w, sgu_b, w_out, w_pw1, b_pw1, w_dw, b_dw, conv_ln_gain, conv_ln_bias, w_pw2, b_pw2, w_query, sub_keys, expert_u, expert_v):
    P = dict(norm_gain=norm_gain, w_ada=w_ada, b_ada=b_ada, rel_bias=rel_bias, w_in=w_in,
             q_gain=q_gain, k_gain=k_gain, lam_q1=lam_q1, lam_k1=lam_k1, lam_q2=lam_q2, lam_k2=lam_k2,
             subln_gain=subln_gain, sgu_ln_gain=sgu_ln_gain, sgu_ln_bias=sgu_ln_bias, sgu_w=sgu_w,
             sgu_b=sgu_b, w_out=w_out, w_pw1=w_pw1, b_pw1=b_pw1, w_dw=w_dw, b_dw=b_dw,
             conv_ln_gain=conv_ln_gain, conv_ln_bias=conv_ln_bias, w_pw2=w_pw2, b_pw2=b_pw2,
             w_query=w_query, sub_keys=sub_keys, expert_u=expert_u, expert_v=expert_v)
    y_prompt, k_prompt, v_prompt, _, conv_prompt = trunk(x_prompt, c_prompt, P, None, None, None, None)
    y_sample, k_sample, v_sample, sgu_v_sample, conv_sample = trunk(
        x_sample, c_sample, P, cache_k, cache_v, state_conv, page_table)
    return (y_prompt, y_sample, k_prompt, v_prompt, k_sample, v_sample, sgu_v_sample, conv_prompt, conv_sample)
```

```python
import functools
import math

import jax
import jax.numpy as jnp
from jax import lax
from jax.experimental import pallas as pl
from jax.experimental.pallas import tpu as pltpu

D_MODEL = 4096
DEPTH = 4
PAGE_SIZE = 128
DH = 128
D_ATT = D_MODEL // 2
H_A = D_ATT // (2 * DH)
ATTN_SCALE = DH ** -0.5
Q_BLOCK = 128
N_BUCKETS = 32
MAX_DISTANCE = 128
D_SGU = D_MODEL - D_ATT
SGU_GROUPS = 8
SGU_CG = D_SGU // SGU_GROUPS
CHUNK = 128
D_CONV = D_MODEL
CONV_W = 31
PEER_HEADS = 8
N_KEYS = 128
PEER_TOPK = 16
D_KEY = 256
PEER_BLOCK = 64
RMS_EPS = 1e-6
LN_EPS = 1e-5

VMEM_LIMIT_BYTES = 48 * 1024 * 1024


def _mm_kernel(x_ref, w_ref, b_ref, o_ref, acc_ref):
    k = pl.program_id(2)

    @pl.when(k == 0)
    def _():
        acc_ref[...] = jnp.zeros_like(acc_ref)

    x, w = x_ref[...], w_ref[...]
    if x.dtype != jnp.bfloat16:
        x = x.astype(jnp.bfloat16)
    if w.dtype != jnp.bfloat16:
        w = w.astype(jnp.bfloat16)
    acc_ref[...] += jnp.dot(x, w, preferred_element_type=jnp.float32)

    @pl.when(k == pl.num_programs(2) - 1)
    def _():
        o_ref[...] = acc_ref[...] + b_ref[...]


def _pick(dim, pref):
    for c in pref:
        if dim % c == 0:
            return c
    return dim


def mm(x, w, b=None):
    m, kdim = x.shape
    n = w.shape[1]
    x = x.astype(jnp.bfloat16)
    mp = -(-m // 16) * 16
    if mp != m:
        x = jnp.pad(x, ((0, mp - m), (0, 0)))
    if b is None:
        b = jnp.zeros((n,), jnp.float32)
    bm = _pick(mp, (2048, 1024, 512, 256, 128, 64, 32, 16))
    bn = _pick(n, (1024, 512, 256, 128))
    bk = _pick(kdim, (1024, 512, 256, 128))
    out = pl.pallas_call(
        _mm_kernel,
        grid=(mp // bm, n // bn, kdim // bk),
        in_specs=[pl.BlockSpec((bm, bk), lambda i, j, k: (i, k)),
                  pl.BlockSpec((bk, bn), lambda i, j, k: (k, j)),
                  pl.BlockSpec((1, bn), lambda i, j, k: (0, j))],
        out_specs=pl.BlockSpec((bm, bn), lambda i, j, k: (i, j)),
        out_shape=jax.ShapeDtypeStruct((mp, n), jnp.float32),
        scratch_shapes=[pltpu.VMEM((bm, bn), jnp.float32)],
        compiler_params=pltpu.CompilerParams(
            dimension_semantics=("parallel", "parallel", "arbitrary"),
            vmem_limit_bytes=VMEM_LIMIT_BYTES),
        name="mm",
    )(x, w, b.reshape(1, n))
    return out[:m]


def mm3(x, w, b=None):
    bsz, t, d = x.shape
    return mm(x.reshape(bsz * t, d), w, b).reshape(bsz, t, w.shape[1])


def _top16_cols(s, n_rows):
    rows = lax.broadcasted_iota(jnp.int32, s.shape, 0).astype(jnp.float32)
    vals, ids = [], []
    for _ in range(PEER_TOPK):
        m = jnp.max(s, axis=0, keepdims=True)
        idx = jnp.min(jnp.where(s == m, rows, float(n_rows)), axis=0, keepdims=True)
        s = jnp.where(rows == idx, -jnp.inf, s)
        vals.append(m)
        ids.append(idx)
    return jnp.concatenate(vals, axis=0), jnp.concatenate(ids, axis=0)


def _route_kernel(q_ref, sk_ref, eid_ref, gate_ref):
    half = D_KEY // 2
    nt = (((1,), (1,)), ((), ()))
    sv, si = [], []
    for c in range(2):
        qc = q_ref[:, c * half:(c + 1) * half].astype(jnp.bfloat16)
        s = lax.dot_general(sk_ref[c].astype(jnp.bfloat16), qc, nt,
                            preferred_element_type=jnp.float32)
        v, i = _top16_cols(s, N_KEYS)
        sv.append(v)
        si.append(i)
    cand = jnp.concatenate([sv[0][i:i + 1, :] + sv[1] for i in range(PEER_TOPK)], axis=0)
    cid = jnp.concatenate([si[0][i:i + 1, :] * float(N_KEYS) + si[1] for i in range(PEER_TOPK)], axis=0)
    rows = lax.broadcasted_iota(jnp.int32, cand.shape, 0).astype(jnp.float32)
    n_cand = PEER_TOPK * PEER_TOPK
    best, eids = [], []
    for _ in range(PEER_TOPK):
        m = jnp.max(cand, axis=0, keepdims=True)
        pos = jnp.min(jnp.where(cand == m, rows, float(n_cand)), axis=0, keepdims=True)
        hit = rows == pos
        eids.append(jnp.sum(jnp.where(hit, cid, 0.0), axis=0, keepdims=True))
        cand = jnp.where(hit, -jnp.inf, cand)
        best.append(m)
    best = jnp.concatenate(best, axis=0)
    e = jnp.exp(best - best[0:1, :])
    gate_ref[0] = e / jnp.sum(e, axis=0, keepdims=True)
    eid_ref[0] = jnp.concatenate(eids, axis=0).astype(jnp.int32)


def peer_route(q, sk):
    n = q.shape[0]
    tb = min(128, n)
    eid, gate = pl.pallas_call(
        _route_kernel,
        grid=(n // tb, PEER_HEADS),
        in_specs=[pl.BlockSpec((tb, D_KEY), lambda i, h: (i, h)),
                  pl.BlockSpec((2, N_KEYS, D_KEY // 2), lambda i, h: (0, 0, 0))],
        out_specs=[pl.BlockSpec((1, PEER_TOPK, tb), lambda i, h: (h, 0, i)),
                   pl.BlockSpec((1, PEER_TOPK, tb), lambda i, h: (h, 0, i))],
        out_shape=[jax.ShapeDtypeStruct((PEER_HEADS, PEER_TOPK, n), jnp.int32),
                   jax.ShapeDtypeStruct((PEER_HEADS, PEER_TOPK, n), jnp.float32)],
        compiler_params=pltpu.CompilerParams(dimension_semantics=("parallel", "parallel")),
        name="peer_route",
    )(q, sk)
    n_sel = PEER_HEADS * PEER_TOPK
    return eid.reshape(n_sel, n).T, gate.reshape(n_sel, n).T


PEER_GROUP = 8
PEER_RING = 2
N_SEL = PEER_HEADS * PEER_TOPK
SUBLANES = 8
LANES = 128


def _gelu_exact(x):
    return 0.5 * x * (1.0 + lax.erf(x * (2.0 ** -0.5)))


def pack_experts(eu, ev):
    lo = lax.bitcast_convert_type(eu.astype(jnp.bfloat16), jnp.uint16).astype(jnp.uint32)
    hi = lax.bitcast_convert_type(ev.astype(jnp.bfloat16), jnp.uint16).astype(jnp.uint32)
    return ((hi << 16) | lo)[:, None, :]


def _peer_kernel(eid0, eid1, eid2, x_ref, gate_ref, w_hbm, out_ref, wbuf, acol, xbs, sem):
    g = pl.program_id(0)
    n_groups = pl.num_programs(0)
    s = lax.rem(g, PEER_RING)
    n_tiles = N_SEL // SUBLANES
    n_lane_tiles = D_MODEL // LANES

    def row_copy(eid_ref, et, j, b, t):
        return pltpu.make_async_copy(w_hbm.at[eid_ref[0, et, j]], wbuf.at[b, t, pl.ds(j, 1), :], sem.at[b, t])

    def wait_slot(b, t):
        pltpu.make_async_copy(wbuf.at[b, t], wbuf.at[b, t], sem.at[b, t]).wait()

    @pl.when(g == 0)
    def _():
        def fill(eid_ref, b, n_tok):
            def body(t, carry):
                for j in range(N_SEL):
                    row_copy(eid_ref, t, j, b, t).start()
                return carry
            lax.fori_loop(0, n_tok, body, 0)
        fill(eid0, 0, PEER_GROUP)
        fill(eid1, 1, PEER_GROUP - 1)

    sub_iota = lax.broadcasted_iota(jnp.int32, (SUBLANES, LANES), 0)
    lane_iota = lax.broadcasted_iota(jnp.int32, (SUBLANES, LANES), 1)
    hi_mask = jnp.uint32(0xFFFF0000)
    per_step = N_SEL // (2 * n_tiles)

    def tree_sum(parts):
        while len(parts) > 1:
            parts = [a + b for a, b in zip(parts[0::2], parts[1::2])]
        return parts[0]

    rows = []
    for t in range(PEER_GROUP):
        wait_slot(s, t)
        if t == 0:
            refill = (eid1, PEER_GROUP - 1, 1 - s, PEER_GROUP - 1)
        else:
            refill = (eid2, t - 1, s, t - 1)

        def start_some(step, refill=refill):
            eref, et, b, bt = refill
            for j in range(step * per_step, (step + 1) * per_step):
                row_copy(eref, et, j, b, bt).start()

        xbs[...] = jnp.broadcast_to(x_ref[t:t + 1, :], (SUBLANES, D_MODEL))
        gb = jnp.broadcast_to(gate_ref[t:t + 1, :], (SUBLANES, N_SEL))
        for r in range(n_tiles):
            w = wbuf[s, t, r * SUBLANES:(r + 1) * SUBLANES, :]
            xb = xbs[...]
            start_some(r)
            prod = lax.bitcast_convert_type(w << 16, jnp.float32) * xb
            part = tree_sum([prod[:, c * LANES:(c + 1) * LANES] for c in range(n_lane_tiles)])
            h = jnp.sum(part, axis=1, keepdims=True)
            gcol = jnp.sum(jnp.where(lane_iota == sub_iota + r * SUBLANES, gb, 0.0), axis=1, keepdims=True)
            acol[r * SUBLANES:(r + 1) * SUBLANES, :] = jnp.broadcast_to(_gelu_exact(h) * gcol, (SUBLANES, LANES))
        acc = jnp.zeros((SUBLANES, D_MODEL), jnp.float32)
        for r in range(n_tiles):
            w = wbuf[s, t, r * SUBLANES:(r + 1) * SUBLANES, :]
            a = acol[r * SUBLANES:(r + 1) * SUBLANES, :]
            start_some(n_tiles + r)
            acc = acc + (lax.bitcast_convert_type(w & hi_mask, jnp.float32)
                         * jnp.concatenate([a] * n_lane_tiles, axis=1))
        rows.append(jnp.sum(acc, axis=0, keepdims=True))
    out_ref[...] = jnp.concatenate(rows, axis=0)

    @pl.when(g == n_groups - 1)
    def _():
        for t in range(PEER_GROUP):
            wait_slot(1 - s, t)
        for t in range(PEER_GROUP - 1):
            wait_slot(s, t)


def peer_experts(x, eid, gate, w_packed):
    n = x.shape[0]
    n_groups = n // PEER_GROUP
    assert n % PEER_GROUP == 0 and n_groups >= PEER_RING
    eid3 = eid.reshape(n_groups, PEER_GROUP, N_SEL)
    smem_block = functools.partial(pl.BlockSpec, (1, PEER_GROUP, N_SEL), memory_space=pltpu.SMEM)
    return pl.pallas_call(
        _peer_kernel,
        grid=(n_groups,),
        in_specs=[smem_block(lambda g: (g, 0, 0)),
                  smem_block(lambda g: (jnp.minimum(g + 1, n_groups - 1), 0, 0)),
                  smem_block(lambda g: (jnp.minimum(g + 2, n_groups - 1), 0, 0)),
                  pl.BlockSpec((PEER_GROUP, D_MODEL), lambda g: (g, 0)),
                  pl.BlockSpec((PEER_GROUP, N_SEL), lambda g: (g, 0)),
                  pl.BlockSpec(memory_space=pl.ANY)],
        out_specs=pl.BlockSpec((PEER_GROUP, D_MODEL), lambda g: (g, 0)),
        out_shape=jax.ShapeDtypeStruct((n, D_MODEL), jnp.float32),
        scratch_shapes=[pltpu.VMEM((PEER_RING, PEER_GROUP, N_SEL, D_MODEL), jnp.uint32),
                        pltpu.VMEM((N_SEL, LANES), jnp.float32),
                        pltpu.VMEM((SUBLANES, D_MODEL), jnp.float32),
                        pltpu.SemaphoreType.DMA((PEER_RING, PEER_GROUP))],
        compiler_params=pltpu.CompilerParams(dimension_semantics=("arbitrary",),
                                             vmem_limit_bytes=VMEM_LIMIT_BYTES),
        name="peer_experts",
    )(eid3, eid3, eid3, x, gate, w_packed)


ATT_BLOCK = 256
FAR_BUCKET_MIN_DIST = MAX_DISTANCE


def _attn_kernel(lam_ref, bfar_ref, q_ref, k_ref, v_ref, bd_ref, bs_ref, g_ref, o_ref, m_ref, l_ref, acc_ref,
                 *, out_scale):
    h = pl.program_id(1)
    i = pl.program_id(2)
    nt = (((1,), (1,)), ((), ()))
    m_ref[...] = jnp.full(m_ref.shape, -jnp.inf, jnp.float32)
    l_ref[...] = jnp.zeros(l_ref.shape, jnp.float32)
    acc_ref[...] = jnp.zeros(acc_ref.shape, jnp.float32)

    def step(start, bias):
        kb = k_ref[pl.ds(start, ATT_BLOCK), :]
        vb = v_ref[pl.ds(start, ATT_BLOCK), :]
        for c in range(2):
            s = lax.dot_general(q_ref[:, c * DH:(c + 1) * DH], kb[:, c * DH:(c + 1) * DH], nt,
                                preferred_element_type=jnp.float32) * ATTN_SCALE + bias
            m_old = m_ref[c]
            m_new = jnp.maximum(m_old, jnp.max(s, axis=1, keepdims=True))
            alpha = jnp.exp(m_old - m_new)
            p = jnp.exp(s - m_new[:, 0:1])
            l_ref[c] = alpha * l_ref[c] + jnp.sum(p, axis=1, keepdims=True)
            acc_ref[c] = acc_ref[c] * alpha[:, 0:1] + jnp.dot(p.astype(jnp.bfloat16), vb,
                                                             preferred_element_type=jnp.float32)
            m_ref[c] = m_new

    def far_body(j, carry):
        step(pl.multiple_of(j * ATT_BLOCK, ATT_BLOCK), bfar_ref[h])
        return carry

    lax.fori_loop(0, jnp.maximum(i - 1, 0), far_body, 0)

    @pl.when(i >= 1)
    def _():
        step(pl.multiple_of((i - 1) * ATT_BLOCK, ATT_BLOCK), bs_ref[0])

    step(pl.multiple_of(i * ATT_BLOCK, ATT_BLOCK), bd_ref[0])

    o = acc_ref[0] / l_ref[0][:, 0:1] - lam_ref[0] * (acc_ref[1] / l_ref[1][:, 0:1])
    y = o * lax.rsqrt(jnp.mean(o * o, axis=1, keepdims=True) + RMS_EPS)
    o_ref[...] = y * g_ref[...] * out_scale


def prompt_attention(q, k, v, lam, rel_bias, subln_gain, out_scale, seq):
    n, width = q.shape
    n_heads = width // (2 * DH)
    assert ATT_BLOCK >= FAR_BUCKET_MIN_DIST and seq % ATT_BLOCK == 0
    r = jnp.arange(ATT_BLOCK, dtype=jnp.int32)
    d_diag = r[:, None] - r[None, :]
    bias_diag = jnp.where(d_diag >= 0, jnp.moveaxis(rel_bias[rel_bucket(d_diag)], -1, 0), -jnp.inf)
    bias_sub = jnp.moveaxis(rel_bias[rel_bucket(d_diag + ATT_BLOCK)], -1, 0)
    bias_far = rel_bias[N_BUCKETS - 1]
    nq = seq // ATT_BLOCK
    smem = pl.BlockSpec(memory_space=pltpu.SMEM)
    return pl.pallas_call(
        functools.partial(_attn_kernel, out_scale=out_scale),
        grid=(n // seq, n_heads, nq),
        in_specs=[smem, smem,
                  pl.BlockSpec((ATT_BLOCK, 2 * DH), lambda b, h, i: (b * nq + i, h)),
                  pl.BlockSpec((seq, 2 * DH), lambda b, h, i: (b, h)),
                  pl.BlockSpec((seq, 2 * DH), lambda b, h, i: (b, h)),
                  pl.BlockSpec((1, ATT_BLOCK, ATT_BLOCK), lambda b, h, i: (h, 0, 0)),
                  pl.BlockSpec((1, ATT_BLOCK, ATT_BLOCK), lambda b, h, i: (h, 0, 0)),
                  pl.BlockSpec((1, 2 * DH), lambda b, h, i: (0, 0))],
        out_specs=pl.BlockSpec((ATT_BLOCK, 2 * DH), lambda b, h, i: (b * nq + i, h)),
        out_shape=jax.ShapeDtypeStruct((n, width), jnp.float32),
        scratch_shapes=[pltpu.VMEM((2, ATT_BLOCK, LANES), jnp.float32),
                        pltpu.VMEM((2, ATT_BLOCK, LANES), jnp.float32),
                        pltpu.VMEM((2, ATT_BLOCK, 2 * DH), jnp.float32)],
        compiler_params=pltpu.CompilerParams(dimension_semantics=("parallel", "parallel", "parallel"),
                                             vmem_limit_bytes=VMEM_LIMIT_BYTES),
        name="prompt_attention",
    )(lam.reshape(1), bias_far, q, k, v, bias_diag, bias_sub, subln_gain.reshape(1, 2 * DH))


def rmsnorm(x, g):
    y = x * lax.rsqrt(jnp.mean(x * x, axis=-1, keepdims=True) + RMS_EPS)
    return y * g


def layernorm(x, g, b):
    mu = jnp.mean(x, axis=-1, keepdims=True)
    xc = x - mu
    var = jnp.mean(xc * xc, axis=-1, keepdims=True)
    return xc * lax.rsqrt(var + LN_EPS) * g + b


def rel_bucket(dist):
    n = jnp.maximum(dist, 0)
    max_exact = N_BUCKETS // 2
    nf = jnp.maximum(n, 1).astype(jnp.float32)
    large = max_exact + (jnp.log(nf / max_exact) / math.log(MAX_DISTANCE / max_exact)
                         * (N_BUCKETS - max_exact)).astype(jnp.int32)
    large = jnp.minimum(large, N_BUCKETS - 1)
    return jnp.where(n < max_exact, n, large)


def diff_attend(q, k, v, q_pos, k_pos, lam, rel_bias):
    dist = q_pos[:, None] - k_pos[None, :]
    bias = jnp.moveaxis(rel_bias[rel_bucket(dist)], -1, 0).astype(jnp.float32)
    s = jnp.einsum('bqhcd,bkhcd->bhcqk', q, k).astype(jnp.float32) * ATTN_SCALE + bias[None, :, None]
    s = jnp.where(dist >= 0, s, -jnp.inf)
    p = jax.nn.softmax(s, axis=-1)
    w = p[:, :, 0] - lam * p[:, :, 1]
    return jnp.einsum('bhqk,bkhe->bqhe', w.astype(v.dtype), v)


def chunk_mix(g, w_s, b_s):
    B, T, G, C = g.shape
    pad = (-T) % CHUNK
    gc = jnp.pad(g, ((0, 0), (0, pad), (0, 0), (0, 0))).reshape(B, (T + pad) // CHUNK, CHUNK, G, C)
    w = jnp.where(jnp.tril(jnp.ones((CHUNK, CHUNK), dtype=bool)), w_s, 0.0)
    m = jnp.einsum('gts,bnsgc->bntgc', w.astype(g.dtype), gc) + b_s.T[None, None, :, :, None].astype(g.dtype)
    return m.reshape(B, T + pad, G, C)[:, :T]


def even_mixer(xm, i, layer_idx, P, past_k, past_v):
    B, T, _ = xm.shape
    f32 = jnp.float32
    proj = mm3(xm, P['w_in'][i])
    q, k, v, u, g = jnp.split(proj, [D_ATT, 2 * D_ATT, 3 * D_ATT, 3 * D_ATT + D_SGU], axis=-1)
    q = rmsnorm(q.reshape(B, T, H_A, 2, DH), P['q_gain'][i])
    k = rmsnorm(k.reshape(B, T, H_A, 2, DH), P['k_gain'][i])
    v = v.reshape(B, T, H_A, 2 * DH)
    lam_init = 0.8 - 0.6 * math.exp(-0.3 * layer_idx)
    lam = (jnp.exp(jnp.sum(P['lam_q1'][i].astype(f32) * P['lam_k1'][i].astype(f32)))
           - jnp.exp(jnp.sum(P['lam_q2'][i].astype(f32) * P['lam_k2'][i].astype(f32))) + lam_init)
    if past_k is None:
        bf16 = jnp.bfloat16
        attn = prompt_attention(q.reshape(B * T, D_ATT).astype(bf16), k.reshape(B * T, D_ATT).astype(bf16),
                                v.reshape(B * T, D_ATT).astype(bf16), lam, P['rel_bias'], P['subln_gain'][i],
                                1.0 - lam_init, T).reshape(B, T, D_ATT)
    else:
        past = past_k.shape[1]
        k_all = jnp.concatenate([past_k.astype(k.dtype), k], axis=1)
        v_all = jnp.concatenate([past_v.astype(v.dtype), v], axis=1)
        o = diff_attend(q, k_all, v_all, past + jnp.arange(T, dtype=jnp.int32),
                        jnp.arange(past + T, dtype=jnp.int32), lam, P['rel_bias'])
        attn = (rmsnorm(o, P['subln_gain'][i]) * (1.0 - lam_init)).reshape(B, T, D_ATT)
    u = jax.nn.gelu(u, approximate=False)
    g = layernorm(jax.nn.gelu(g, approximate=False), P['sgu_ln_gain'][i], P['sgu_ln_bias'][i])
    s_out = u * chunk_mix(g.reshape(B, T, SGU_GROUPS, SGU_CG), P['sgu_w'][i], P['sgu_b'][i]).reshape(B, T, D_SGU)
    y = mm3(jnp.concatenate([attn, s_out], axis=-1), P['w_out'][i])
    return y, k.reshape(B, T, H_A, 2 * DH), v, g


def conv_mixer(xm, i, P, buf):
    a = mm3(xm, P['w_pw1'][i], P['b_pw1'][i])
    val, gate = jnp.split(a, 2, axis=-1)
    glu = val * jax.nn.sigmoid(gate)
    full = jnp.concatenate([buf.astype(glu.dtype), glu], axis=1)
    h = lax.conv_general_dilated(full, P['w_dw'][i][:, None, :].astype(full.dtype), (1,), 'VALID',
                                 dimension_numbers=('NWC', 'WIO', 'NWC'),
                                 feature_group_count=D_CONV) + P['b_dw'][i]
    h = jax.nn.silu(layernorm(h, P['conv_ln_gain'][i], P['conv_ln_bias'][i]))
    return mm3(h, P['w_pw2'][i], P['b_pw2'][i]), full[:, -(CONV_W - 1):]


def peer(xm, l, P):
    B, T, D = xm.shape
    n = B * T
    xf = xm.reshape(n, D)
    eid, gate = peer_route(mm(xf, P['w_query'][l]), P['sub_keys'][l])
    return peer_experts(xf, eid, gate, P['experts'][l]).reshape(B, T, D)


def trunk(x, mods, P, cache_k, cache_v, conv_state, page_table):
    B = x.shape[0]
    sample = page_table is not None
    new_k, new_v, new_g, new_conv = [], [], [], []
    for l in range(DEPTH):
        sh1, sc1, gt1, sh2, sc2, gt2 = jnp.split(mods[l][:, None, :], 6, axis=-1)
        xm = rmsnorm(x, P['norm_gain'][l, 0]) * (1 + sc1) + sh1
        i = l // 2
        if l % 2 == 0:
            if sample:
                past = page_table.shape[1] * PAGE_SIZE
                pk = cache_k[i][page_table].reshape(B, past, H_A, 2, DH)
                pv = cache_v[i][page_table].reshape(B, past, H_A, 2 * DH)
            else:
                pk, pv = None, None
            mix, kr, vr, gr = even_mixer(xm, i, l, P, pk, pv)
            new_k.append(kr)
            new_v.append(vr)
            if sample:
                new_g.append(gr)
        else:
            buf = conv_state[i] if sample else jnp.zeros((B, CONV_W - 1, D_CONV), x.dtype)
            mix, cb = conv_mixer(xm, i, P, buf)
            new_conv.append(cb)
        x = x + gt1 * mix
        hm = rmsnorm(x, P['norm_gain'][l, 1]) * (1 + sc2) + sh2
        x = x + gt2 * peer(hm, l, P)
    g_rows = jnp.stack(new_g) if sample else None
    return x, jnp.stack(new_k), jnp.stack(new_v), g_rows, jnp.stack(new_conv)


def kernel(x_prompt, x_sample, c_prompt, c_sample, cache_k, cache_v, state_conv, page_table, norm_gain, w_ada, b_ada, rel_bias, w_in, q_gain, k_gain, lam_q1, lam_k1, lam_q2, lam_k2, subln_gain, sgu_ln_gain, sgu_ln_bias, sgu_w, sgu_b, w_out, w_pw1, b_pw1, w_dw, b_dw, conv_ln_gain, conv_ln_bias, w_pw2, b_pw2, w_query, sub_keys, expert_u, expert_v):
    bf16 = jnp.bfloat16
    P = dict(norm_gain=norm_gain, rel_bias=rel_bias, w_in=w_in.astype(bf16),
             q_gain=q_gain, k_gain=k_gain, lam_q1=lam_q1, lam_k1=lam_k1, lam_q2=lam_q2, lam_k2=lam_k2,
             subln_gain=subln_gain, sgu_ln_gain=sgu_ln_gain, sgu_ln_bias=sgu_ln_bias, sgu_w=sgu_w,
             sgu_b=sgu_b, w_out=w_out.astype(bf16), w_pw1=w_pw1.astype(bf16), b_pw1=b_pw1, w_dw=w_dw, b_dw=b_dw,
             conv_ln_gain=conv_ln_gain, conv_ln_bias=conv_ln_bias, w_pw2=w_pw2.astype(bf16), b_pw2=b_pw2,
             w_query=w_query.astype(bf16), sub_keys=sub_keys,
             experts=[pack_experts(expert_u[l], expert_v[l]) for l in range(DEPTH)])
    n_prompt = c_prompt.shape[0]
    cs = jax.nn.silu(jnp.concatenate([c_prompt, c_sample], axis=0))
    mods = [mm(cs, w_ada[l], b_ada[l]) for l in range(DEPTH)]
    y_prompt, k_prompt, v_prompt, _, conv_prompt = trunk(
        x_prompt, [m[:n_prompt] for m in mods], P, None, None, None, None)
    y_sample, k_sample, v_sample, sgu_v_sample, conv_sample = trunk(
        x_sample, [m[n_prompt:] for m in mods], P, cache_k, cache_v, state_conv, page_table)
    return (y_prompt, y_sample, k_prompt, v_prompt, k_sample, v_sample, sgu_v_sample, conv_prompt, conv_sample)
```

```python
import functools
import math

import jax
import jax.numpy as jnp
from jax import lax
from jax.experimental import pallas as pl
from jax.experimental.pallas import tpu as pltpu

D_MODEL = 4096
DEPTH = 4
PAGE_SIZE = 128
DH = 128
D_ATT = D_MODEL // 2
H_A = D_ATT // (2 * DH)
ATTN_SCALE = DH ** -0.5
Q_BLOCK = 128
N_BUCKETS = 32
MAX_DISTANCE = 128
D_SGU = D_MODEL - D_ATT
SGU_GROUPS = 8
SGU_CG = D_SGU // SGU_GROUPS
CHUNK = 128
D_CONV = D_MODEL
CONV_W = 31
PEER_HEADS = 8
N_KEYS = 128
PEER_TOPK = 16
D_KEY = 256
PEER_BLOCK = 64
RMS_EPS = 1e-6
LN_EPS = 1e-5

VMEM_LIMIT_BYTES = 48 * 1024 * 1024


def _mm_kernel(x_ref, w_ref, b_ref, o_ref, acc_ref):
    k = pl.program_id(2)

    @pl.when(k == 0)
    def _():
        acc_ref[...] = jnp.zeros_like(acc_ref)

    x, w = x_ref[...], w_ref[...]
    if x.dtype != jnp.bfloat16:
        x = x.astype(jnp.bfloat16)
    if w.dtype != jnp.bfloat16:
        w = w.astype(jnp.bfloat16)
    acc_ref[...] += jnp.dot(x, w, preferred_element_type=jnp.float32)

    @pl.when(k == pl.num_programs(2) - 1)
    def _():
        o_ref[...] = acc_ref[...] + b_ref[...]


def _pick(dim, pref):
    for c in pref:
        if dim % c == 0:
            return c
    return dim


def mm(x, w, b=None):
    m, kdim = x.shape
    n = w.shape[1]
    x = x.astype(jnp.bfloat16)
    mp = -(-m // 16) * 16
    if mp != m:
        x = jnp.pad(x, ((0, mp - m), (0, 0)))
    if b is None:
        b = jnp.zeros((n,), jnp.float32)
    bm = _pick(mp, (2048, 1024, 512, 256, 128, 64, 32, 16))
    bn = _pick(n, (1024, 512, 256, 128))
    bk = _pick(kdim, (1024, 512, 256, 128))
    out = pl.pallas_call(
        _mm_kernel,
        grid=(mp // bm, n // bn, kdim // bk),
        in_specs=[pl.BlockSpec((bm, bk), lambda i, j, k: (i, k)),
                  pl.BlockSpec((bk, bn), lambda i, j, k: (k, j)),
                  pl.BlockSpec((1, bn), lambda i, j, k: (0, j))],
        out_specs=pl.BlockSpec((bm, bn), lambda i, j, k: (i, j)),
        out_shape=jax.ShapeDtypeStruct((mp, n), jnp.float32),
        scratch_shapes=[pltpu.VMEM((bm, bn), jnp.float32)],
        compiler_params=pltpu.CompilerParams(
            dimension_semantics=("parallel", "parallel", "arbitrary"),
            vmem_limit_bytes=VMEM_LIMIT_BYTES),
        name="mm",
    )(x, w, b.reshape(1, n))
    return out[:m]


def mm3(x, w, b=None):
    bsz, t, d = x.shape
    return mm(x.reshape(bsz * t, d), w, b).reshape(bsz, t, w.shape[1])


def _top16_cols(s, n_rows):
    rows = lax.broadcasted_iota(jnp.int32, s.shape, 0).astype(jnp.float32)
    vals, ids = [], []
    for _ in range(PEER_TOPK):
        m = jnp.max(s, axis=0, keepdims=True)
        idx = jnp.min(jnp.where(s == m, rows, float(n_rows)), axis=0, keepdims=True)
        s = jnp.where(rows == idx, -jnp.inf, s)
        vals.append(m)
        ids.append(idx)
    return jnp.concatenate(vals, axis=0), jnp.concatenate(ids, axis=0)


def _route_kernel(q_ref, sk_ref, eid_ref, gate_ref):
    half = D_KEY // 2
    nt = (((1,), (1,)), ((), ()))
    sv, si = [], []
    for c in range(2):
        qc = q_ref[:, c * half:(c + 1) * half].astype(jnp.bfloat16)
        s = lax.dot_general(sk_ref[c].astype(jnp.bfloat16), qc, nt,
                            preferred_element_type=jnp.float32)
        v, i = _top16_cols(s, N_KEYS)
        sv.append(v)
        si.append(i)
    t_lanes = sv[0].shape[1]
    row8 = lax.broadcasted_iota(jnp.int32, (SUBLANES, t_lanes), 0)

    def staircase(a0, a1, combine, fill):
        def bc(x, i):
            return jnp.broadcast_to(x[i:i + 1, :], (SUBLANES, t_lanes))
        lo8 = a1[0:SUBLANES]
        a56 = jnp.where(row8 < 2, bc(a0, 5), jnp.where(row8 < 4, bc(a0, 6), bc(a0, 7)))
        b56 = jnp.where(row8 % 2 == 0, bc(a1, 0), bc(a1, 1))
        return jnp.concatenate([
            combine(bc(a0, 0), lo8),
            combine(bc(a0, 0), a1[SUBLANES:2 * SUBLANES]),
            combine(bc(a0, 1), lo8),
            jnp.where(row8 < 5, combine(bc(a0, 2), lo8), fill),
            jnp.where(row8 < 4, combine(bc(a0, 3), lo8), fill),
            jnp.where(row8 < 3, combine(bc(a0, 4), lo8), fill),
            jnp.where(row8 < 6, combine(a56, b56), fill),
            combine(a0[SUBLANES:2 * SUBLANES], bc(a1, 0)),
        ], axis=0)

    cand = staircase(sv[0], sv[1], lambda a, b: a + b, -jnp.inf)
    cid = staircase(si[0], si[1], lambda a, b: a * float(N_KEYS) + b, 0.0)
    rows = lax.broadcasted_iota(jnp.int32, cand.shape, 0).astype(jnp.float32)
    n_cand = cand.shape[0]
    best, eids = [], []
    for _ in range(PEER_TOPK):
        m = jnp.max(cand, axis=0, keepdims=True)
        pos = jnp.min(jnp.where(cand == m, rows, float(n_cand)), axis=0, keepdims=True)
        hit = rows == pos
        eids.append(jnp.sum(jnp.where(hit, cid, 0.0), axis=0, keepdims=True))
        cand = jnp.where(hit, -jnp.inf, cand)
        best.append(m)
    best = jnp.concatenate(best, axis=0)
    e = jnp.exp(best - best[0:1, :])
    gate_ref[0] = e / jnp.sum(e, axis=0, keepdims=True)
    eid_ref[0] = jnp.concatenate(eids, axis=0).astype(jnp.int32)


def peer_route(q, sk):
    n = q.shape[0]
    tb = min(128, n)
    eid, gate = pl.pallas_call(
        _route_kernel,
        grid=(n // tb, PEER_HEADS),
        in_specs=[pl.BlockSpec((tb, D_KEY), lambda i, h: (i, h)),
                  pl.BlockSpec((2, N_KEYS, D_KEY // 2), lambda i, h: (0, 0, 0))],
        out_specs=[pl.BlockSpec((1, PEER_TOPK, tb), lambda i, h: (h, 0, i)),
                   pl.BlockSpec((1, PEER_TOPK, tb), lambda i, h: (h, 0, i))],
        out_shape=[jax.ShapeDtypeStruct((PEER_HEADS, PEER_TOPK, n), jnp.int32),
                   jax.ShapeDtypeStruct((PEER_HEADS, PEER_TOPK, n), jnp.float32)],
        compiler_params=pltpu.CompilerParams(dimension_semantics=("parallel", "parallel")),
        name="peer_route",
    )(q, sk)
    n_sel = PEER_HEADS * PEER_TOPK
    return eid.reshape(n_sel, n).T, gate.reshape(n_sel, n).T


PEER_GROUP = 8
PEER_RING = 2
N_SEL = PEER_HEADS * PEER_TOPK
SUBLANES = 8
LANES = 128


def _gelu_exact(x):
    return 0.5 * x * (1.0 + lax.erf(x * (2.0 ** -0.5)))


PACK_ROWS = 64


def _pack_kernel(u_ref, v_ref, o_ref):
    def bf16_bits(x):
        return lax.bitcast_convert_type(x.astype(jnp.bfloat16).astype(jnp.float32), jnp.uint32)
    lo = lax.shift_right_logical(bf16_bits(u_ref[...]), jnp.uint32(16))
    hi = bf16_bits(v_ref[...]) & jnp.uint32(0xFFFF0000)
    o_ref[:, 0, :] = hi | lo


def pack_experts(eu, ev):
    e, d = eu.shape
    spec = pl.BlockSpec((PACK_ROWS, d), lambda i: (i, 0))
    return pl.pallas_call(
        _pack_kernel,
        grid=(e // PACK_ROWS,),
        in_specs=[spec, spec],
        out_specs=pl.BlockSpec((PACK_ROWS, 1, d), lambda i: (i, 0, 0)),
        out_shape=jax.ShapeDtypeStruct((e, 1, d), jnp.uint32),
        compiler_params=pltpu.CompilerParams(dimension_semantics=("parallel",)),
        name="pack_experts",
    )(eu, ev)


def _split3(p):
    hi = p.astype(jnp.bfloat16)
    r1 = p - hi.astype(jnp.float32)
    mid = r1.astype(jnp.bfloat16)
    lo = (r1 - mid.astype(jnp.float32)).astype(jnp.bfloat16)
    return hi, mid, lo


def _peer_kernel(eid0, eid1, x_ref, gate_ref, w_hbm, out_ref, wbuf, part, acol, xbs, sem):
    g = pl.program_id(0)
    n_groups = pl.num_programs(0)
    s = lax.rem(g, PEER_RING)
    n_tiles = N_SEL // SUBLANES
    n_lane_tiles = D_MODEL // LANES
    nt = (((1,), (1,)), ((), ()))

    def row_copy(eid_ref, et, j, b, t):
        return pltpu.make_async_copy(w_hbm.at[eid_ref[0, et, j]], wbuf.at[b, t, pl.ds(j, 1), :], sem.at[b, t])

    def wait_slot(b, t):
        pltpu.make_async_copy(wbuf.at[b, t], wbuf.at[b, t], sem.at[b, t]).wait()

    @pl.when(g == 0)
    def _():
        def fill(eid_ref, b, n_tok):
            def body(t, carry):
                for j in range(N_SEL):
                    row_copy(eid_ref, t, j, b, t).start()
                return carry
            lax.fori_loop(0, n_tok, body, 0)
        fill(eid0, 0, PEER_GROUP)

    sub_iota = lax.broadcasted_iota(jnp.int32, (SUBLANES, LANES), 0)
    lane_iota = lax.broadcasted_iota(jnp.int32, (SUBLANES, LANES), 1)
    hi_mask = jnp.uint32(0xFFFF0000)
    half_d = D_MODEL // 2
    per_tile = N_SEL // (4 * n_tiles)

    def tree_sum(parts):
        while len(parts) > 1:
            parts = [a + b for a, b in zip(parts[0::2], parts[1::2])]
        return parts[0]

    def start_next(tile_step):
        for i in range(tile_step * per_tile, (tile_step + 1) * per_tile):
            row_copy(eid1, i // N_SEL, i % N_SEL, 1 - s, i // N_SEL).start()

    for t in range(PEER_GROUP):
        wait_slot(s, t)
        xbs[...] = jnp.broadcast_to(x_ref[t:t + 1, :], (SUBLANES, D_MODEL))
        for r in range(n_tiles):
            halves = []
            for hf in range(2):
                w = wbuf[s, t, r * SUBLANES:(r + 1) * SUBLANES, hf * half_d:(hf + 1) * half_d]
                xb = xbs[:, hf * half_d:(hf + 1) * half_d]
                start_next(2 * (t * n_tiles + r) + hf)
                prod = lax.bitcast_convert_type(w << 16, jnp.float32) * xb
                halves.append(tree_sum([prod[:, c * LANES:(c + 1) * LANES] for c in range(n_lane_tiles // 2)]))
            part[t, r * SUBLANES:(r + 1) * SUBLANES, :] = halves[0] + halves[1]

    ones = jnp.ones((SUBLANES, LANES), jnp.bfloat16)
    h_rows = jnp.zeros((PEER_GROUP, N_SEL), jnp.float32)
    for t in range(PEER_GROUP):
        h_t = sum(lax.dot_general(ones, piece, nt, preferred_element_type=jnp.float32)
                  for piece in _split3(part[t]))
        h_rows = jnp.where(lax.broadcasted_iota(jnp.int32, h_rows.shape, 0) == t, h_t, h_rows)
    a_rows = _gelu_exact(h_rows) * gate_ref[...]
    for t in range(PEER_GROUP):
        ab = jnp.broadcast_to(a_rows[t:t + 1, :], (SUBLANES, N_SEL))
        for r in range(n_tiles):
            col = jnp.sum(jnp.where(lane_iota == sub_iota + r * SUBLANES, ab, 0.0), axis=1, keepdims=True)
            acol[t, r * SUBLANES:(r + 1) * SUBLANES, :] = jnp.broadcast_to(col, (SUBLANES, LANES))

    rows = []
    for t in range(PEER_GROUP):
        row = []
        for hf in range(2):
            acc = jnp.zeros((SUBLANES, half_d), jnp.float32)
            for r in range(n_tiles):
                w = wbuf[s, t, r * SUBLANES:(r + 1) * SUBLANES, hf * half_d:(hf + 1) * half_d]
                a = acol[t, r * SUBLANES:(r + 1) * SUBLANES, :]
                start_next(2 * PEER_GROUP * n_tiles + (2 * t + hf) * n_tiles + r)
                acc = acc + (lax.bitcast_convert_type(w & hi_mask, jnp.float32)
                             * jnp.concatenate([a] * (n_lane_tiles // 2), axis=1))
            row.append(jnp.sum(acc, axis=0, keepdims=True))
        rows.append(jnp.concatenate(row, axis=1))
    out_ref[...] = jnp.concatenate(rows, axis=0)

    @pl.when(g == n_groups - 1)
    def _():
        for t in range(PEER_GROUP):
            wait_slot(1 - s, t)


def peer_experts(x, eid, gate, w_packed):
    n = x.shape[0]
    n_groups = n // PEER_GROUP
    assert n % PEER_GROUP == 0 and n_groups >= PEER_RING
    eid3 = eid.reshape(n_groups, PEER_GROUP, N_SEL)
    smem_block = functools.partial(pl.BlockSpec, (1, PEER_GROUP, N_SEL), memory_space=pltpu.SMEM)
    return pl.pallas_call(
        _peer_kernel,
        grid=(n_groups,),
        in_specs=[smem_block(lambda g: (g, 0, 0)),
                  smem_block(lambda g: (jnp.minimum(g + 1, n_groups - 1), 0, 0)),
                  pl.BlockSpec((PEER_GROUP, D_MODEL), lambda g: (g, 0)),
                  pl.BlockSpec((PEER_GROUP, N_SEL), lambda g: (g, 0)),
                  pl.BlockSpec(memory_space=pl.ANY)],
        out_specs=pl.BlockSpec((PEER_GROUP, D_MODEL), lambda g: (g, 0)),
        out_shape=jax.ShapeDtypeStruct((n, D_MODEL), jnp.float32),
        scratch_shapes=[pltpu.VMEM((PEER_RING, PEER_GROUP, N_SEL, D_MODEL), jnp.uint32),
                        pltpu.VMEM((PEER_GROUP, N_SEL, LANES), jnp.float32),
                        pltpu.VMEM((PEER_GROUP, N_SEL, LANES), jnp.float32),
                        pltpu.VMEM((SUBLANES, D_MODEL), jnp.float32),
                        pltpu.SemaphoreType.DMA((PEER_RING, PEER_GROUP))],
        compiler_params=pltpu.CompilerParams(dimension_semantics=("arbitrary",),
                                             vmem_limit_bytes=VMEM_LIMIT_BYTES),
        name="peer_experts",
    )(eid3, eid3, x, gate, w_packed)


ATT_BLOCK = 256
FAR_BUCKET_MIN_DIST = MAX_DISTANCE


def _attn_kernel(lam_ref, bfar_ref, q_ref, k_ref, v_ref, bd_ref, bs_ref, g_ref, o_ref, m_ref, l_ref, acc_ref,
                 *, out_scale):
    h = pl.program_id(1)
    i = pl.program_id(2)
    nt = (((1,), (1,)), ((), ()))
    m_ref[...] = jnp.full(m_ref.shape, -jnp.inf, jnp.float32)
    l_ref[...] = jnp.zeros(l_ref.shape, jnp.float32)
    acc_ref[...] = jnp.zeros(acc_ref.shape, jnp.float32)

    def step(start, bias):
        kb = k_ref[pl.ds(start, ATT_BLOCK), :]
        vb = v_ref[pl.ds(start, ATT_BLOCK), :]
        for c in range(2):
            s = lax.dot_general(q_ref[:, c * DH:(c + 1) * DH], kb[:, c * DH:(c + 1) * DH], nt,
                                preferred_element_type=jnp.float32) * ATTN_SCALE + bias
            m_old = m_ref[c]
            m_new = jnp.maximum(m_old, jnp.max(s, axis=1, keepdims=True))
            alpha = jnp.exp(m_old - m_new)
            p = jnp.exp(s - m_new[:, 0:1])
            l_ref[c] = alpha * l_ref[c] + jnp.sum(p, axis=1, keepdims=True)
            acc_ref[c] = acc_ref[c] * alpha[:, 0:1] + jnp.dot(p.astype(jnp.bfloat16), vb,
                                                             preferred_element_type=jnp.float32)
            m_ref[c] = m_new

    def far_body(j, carry):
        step(pl.multiple_of(j * ATT_BLOCK, ATT_BLOCK), bfar_ref[h])
        return carry

    lax.fori_loop(0, jnp.maximum(i - 1, 0), far_body, 0)

    @pl.when(i >= 1)
    def _():
        step(pl.multiple_of((i - 1) * ATT_BLOCK, ATT_BLOCK), bs_ref[0])

    step(pl.multiple_of(i * ATT_BLOCK, ATT_BLOCK), bd_ref[0])

    o = acc_ref[0] / l_ref[0][:, 0:1] - lam_ref[0] * (acc_ref[1] / l_ref[1][:, 0:1])
    y = o * lax.rsqrt(jnp.mean(o * o, axis=1, keepdims=True) + RMS_EPS)
    o_ref[...] = y * g_ref[...] * out_scale


def prompt_attention(q, k, v, lam, rel_bias, subln_gain, out_scale, seq):
    n, width = q.shape
    n_heads = width // (2 * DH)
    assert ATT_BLOCK >= FAR_BUCKET_MIN_DIST and seq % ATT_BLOCK == 0
    r = jnp.arange(ATT_BLOCK, dtype=jnp.int32)
    d_diag = r[:, None] - r[None, :]
    bias_diag = jnp.where(d_diag >= 0, jnp.moveaxis(rel_bias[rel_bucket(d_diag)], -1, 0), -jnp.inf)
    bias_sub = jnp.moveaxis(rel_bias[rel_bucket(d_diag + ATT_BLOCK)], -1, 0)
    bias_far = rel_bias[N_BUCKETS - 1]
    nq = seq // ATT_BLOCK
    smem = pl.BlockSpec(memory_space=pltpu.SMEM)
    return pl.pallas_call(
        functools.partial(_attn_kernel, out_scale=out_scale),
        grid=(n // seq, n_heads, nq),
        in_specs=[smem, smem,
                  pl.BlockSpec((ATT_BLOCK, 2 * DH), lambda b, h, i: (b * nq + i, h)),
                  pl.BlockSpec((seq, 2 * DH), lambda b, h, i: (b, h)),
                  pl.BlockSpec((seq, 2 * DH), lambda b, h, i: (b, h)),
                  pl.BlockSpec((1, ATT_BLOCK, ATT_BLOCK), lambda b, h, i: (h, 0, 0)),
                  pl.BlockSpec((1, ATT_BLOCK, ATT_BLOCK), lambda b, h, i: (h, 0, 0)),
                  pl.BlockSpec((1, 2 * DH), lambda b, h, i: (0, 0))],
        out_specs=pl.BlockSpec((ATT_BLOCK, 2 * DH), lambda b, h, i: (b * nq + i, h)),
        out_shape=jax.ShapeDtypeStruct((n, width), jnp.float32),
        scratch_shapes=[pltpu.VMEM((2, ATT_BLOCK, LANES), jnp.float32),
                        pltpu.VMEM((2, ATT_BLOCK, LANES), jnp.float32),
                        pltpu.VMEM((2, ATT_BLOCK, 2 * DH), jnp.float32)],
        compiler_params=pltpu.CompilerParams(dimension_semantics=("parallel", "parallel", "parallel"),
                                             vmem_limit_bytes=VMEM_LIMIT_BYTES),
        name="prompt_attention",
    )(lam.reshape(1), bias_far, q, k, v, bias_diag, bias_sub, subln_gain.reshape(1, 2 * DH))


def rmsnorm(x, g):
    y = x * lax.rsqrt(jnp.mean(x * x, axis=-1, keepdims=True) + RMS_EPS)
    return y * g


def layernorm(x, g, b):
    mu = jnp.mean(x, axis=-1, keepdims=True)
    xc = x - mu
    var = jnp.mean(xc * xc, axis=-1, keepdims=True)
    return xc * lax.rsqrt(var + LN_EPS) * g + b


def rel_bucket(dist):
    n = jnp.maximum(dist, 0)
    max_exact = N_BUCKETS // 2
    nf = jnp.maximum(n, 1).astype(jnp.float32)
    large = max_exact + (jnp.log(nf / max_exact) / math.log(MAX_DISTANCE / max_exact)
                         * (N_BUCKETS - max_exact)).astype(jnp.int32)
    large = jnp.minimum(large, N_BUCKETS - 1)
    return jnp.where(n < max_exact, n, large)


def diff_attend(q, k, v, q_pos, k_pos, lam, rel_bias):
    dist = q_pos[:, None] - k_pos[None, :]
    bias = jnp.moveaxis(rel_bias[rel_bucket(dist)], -1, 0).astype(jnp.float32)
    s = jnp.einsum('bqhcd,bkhcd->bhcqk', q, k).astype(jnp.float32) * ATTN_SCALE + bias[None, :, None]
    s = jnp.where(dist >= 0, s, -jnp.inf)
    p = jax.nn.softmax(s, axis=-1)
    w = p[:, :, 0] - lam * p[:, :, 1]
    return jnp.einsum('bhqk,bkhe->bqhe', w.astype(v.dtype), v)


def chunk_mix(g, w_s, b_s):
    B, T, G, C = g.shape
    pad = (-T) % CHUNK
    gc = jnp.pad(g, ((0, 0), (0, pad), (0, 0), (0, 0))).reshape(B, (T + pad) // CHUNK, CHUNK, G, C)
    w = jnp.where(jnp.tril(jnp.ones((CHUNK, CHUNK), dtype=bool)), w_s, 0.0)
    m = jnp.einsum('gts,bnsgc->bntgc', w.astype(g.dtype), gc) + b_s.T[None, None, :, :, None].astype(g.dtype)
    return m.reshape(B, T + pad, G, C)[:, :T]


def even_mixer(xm, i, layer_idx, P, past_k, past_v):
    B, T, _ = xm.shape
    f32 = jnp.float32
    proj = mm3(xm, P['w_in'][i])
    q, k, v, u, g = jnp.split(proj, [D_ATT, 2 * D_ATT, 3 * D_ATT, 3 * D_ATT + D_SGU], axis=-1)
    q = rmsnorm(q.reshape(B, T, H_A, 2, DH), P['q_gain'][i])
    k = rmsnorm(k.reshape(B, T, H_A, 2, DH), P['k_gain'][i])
    v = v.reshape(B, T, H_A, 2 * DH)
    lam_init = 0.8 - 0.6 * math.exp(-0.3 * layer_idx)
    lam = (jnp.exp(jnp.sum(P['lam_q1'][i].astype(f32) * P['lam_k1'][i].astype(f32)))
           - jnp.exp(jnp.sum(P['lam_q2'][i].astype(f32) * P['lam_k2'][i].astype(f32))) + lam_init)
    if past_k is None:
        bf16 = jnp.bfloat16
        attn = prompt_attention(q.reshape(B * T, D_ATT).astype(bf16), k.reshape(B * T, D_ATT).astype(bf16),
                                v.reshape(B * T, D_ATT).astype(bf16), lam, P['rel_bias'], P['subln_gain'][i],
                                1.0 - lam_init, T).reshape(B, T, D_ATT)
    else:
        past = past_k.shape[1]
        k_all = jnp.concatenate([past_k.astype(k.dtype), k], axis=1)
        v_all = jnp.concatenate([past_v.astype(v.dtype), v], axis=1)
        o = diff_attend(q, k_all, v_all, past + jnp.arange(T, dtype=jnp.int32),
                        jnp.arange(past + T, dtype=jnp.int32), lam, P['rel_bias'])
        attn = (rmsnorm(o, P['subln_gain'][i]) * (1.0 - lam_init)).reshape(B, T, D_ATT)
    u = jax.nn.gelu(u, approximate=False)
    g = layernorm(jax.nn.gelu(g, approximate=False), P['sgu_ln_gain'][i], P['sgu_ln_bias'][i])
    s_out = u * chunk_mix(g.reshape(B, T, SGU_GROUPS, SGU_CG), P['sgu_w'][i], P['sgu_b'][i]).reshape(B, T, D_SGU)
    y = mm3(jnp.concatenate([attn, s_out], axis=-1), P['w_out'][i])
    return y, k.reshape(B, T, H_A, 2 * DH), v, g


def conv_mixer(xm, i, P, buf):
    a = mm3(xm, P['w_pw1'][i], P['b_pw1'][i])
    val, gate = jnp.split(a, 2, axis=-1)
    glu = val * jax.nn.sigmoid(gate)
    full = jnp.concatenate([buf.astype(glu.dtype), glu], axis=1)
    h = lax.conv_general_dilated(full, P['w_dw'][i][:, None, :].astype(full.dtype), (1,), 'VALID',
                                 dimension_numbers=('NWC', 'WIO', 'NWC'),
                                 feature_group_count=D_CONV) + P['b_dw'][i]
    h = jax.nn.silu(layernorm(h, P['conv_ln_gain'][i], P['conv_ln_bias'][i]))
    return mm3(h, P['w_pw2'][i], P['b_pw2'][i]), full[:, -(CONV_W - 1):]


def peer(xm, l, P):
    B, T, D = xm.shape
    n = B * T
    xf = xm.reshape(n, D)
    eid, gate = peer_route(mm(xf, P['w_query'][l]), P['sub_keys'][l])
    return peer_experts(xf, eid, gate, P['experts'][l]).reshape(B, T, D)


def trunk(x, mods, P, cache_k, cache_v, conv_state, page_table):
    B = x.shape[0]
    sample = page_table is not None
    new_k, new_v, new_g, new_conv = [], [], [], []
    for l in range(DEPTH):
        sh1, sc1, gt1, sh2, sc2, gt2 = jnp.split(mods[l][:, None, :], 6, axis=-1)
        xm = rmsnorm(x, P['norm_gain'][l, 0]) * (1 + sc1) + sh1
        i = l // 2
        if l % 2 == 0:
            if sample:
                past = page_table.shape[1] * PAGE_SIZE
                pk = cache_k[i][page_table].reshape(B, past, H_A, 2, DH)
                pv = cache_v[i][page_table].reshape(B, past, H_A, 2 * DH)
            else:
                pk, pv = None, None
            mix, kr, vr, gr = even_mixer(xm, i, l, P, pk, pv)
            new_k.append(kr)
            new_v.append(vr)
            if sample:
                new_g.append(gr)
        else:
            buf = conv_state[i] if sample else jnp.zeros((B, CONV_W - 1, D_CONV), x.dtype)
            mix, cb = conv_mixer(xm, i, P, buf)
            new_conv.append(cb)
        x = x + gt1 * mix
        hm = rmsnorm(x, P['norm_gain'][l, 1]) * (1 + sc2) + sh2
        x = x + gt2 * peer(hm, l, P)
    g_rows = jnp.stack(new_g) if sample else None
    return x, jnp.stack(new_k), jnp.stack(new_v), g_rows, jnp.stack(new_conv)


def kernel(x_prompt, x_sample, c_prompt, c_sample, cache_k, cache_v, state_conv, page_table, norm_gain, w_ada, b_ada, rel_bias, w_in, q_gain, k_gain, lam_q1, lam_k1, lam_q2, lam_k2, subln_gain, sgu_ln_gain, sgu_ln_bias, sgu_w, sgu_b, w_out, w_pw1, b_pw1, w_dw, b_dw, conv_ln_gain, conv_ln_bias, w_pw2, b_pw2, w_query, sub_keys, expert_u, expert_v):
    bf16 = jnp.bfloat16
    P = dict(norm_gain=norm_gain, rel_bias=rel_bias, w_in=w_in.astype(bf16),
             q_gain=q_gain, k_gain=k_gain, lam_q1=lam_q1, lam_k1=lam_k1, lam_q2=lam_q2, lam_k2=lam_k2,
             subln_gain=subln_gain, sgu_ln_gain=sgu_ln_gain, sgu_ln_bias=sgu_ln_bias, sgu_w=sgu_w,
             sgu_b=sgu_b, w_out=w_out.astype(bf16), w_pw1=w_pw1.astype(bf16), b_pw1=b_pw1, w_dw=w_dw, b_dw=b_dw,
             conv_ln_gain=conv_ln_gain, conv_ln_bias=conv_ln_bias, w_pw2=w_pw2.astype(bf16), b_pw2=b_pw2,
             w_query=w_query.astype(bf16), sub_keys=sub_keys,
             experts=[pack_experts(expert_u[l], expert_v[l]) for l in range(DEPTH)])
    n_prompt = c_prompt.shape[0]
    cs = jax.nn.silu(jnp.concatenate([c_prompt, c_sample], axis=0))
    mods = [mm(cs, w_ada[l], b_ada[l]) for l in range(DEPTH)]
    y_prompt, k_prompt, v_prompt, _, conv_prompt = trunk(
        x_prompt, [m[:n_prompt] for m in mods], P, None, None, None, None)
    y_sample, k_sample, v_sample, sgu_v_sample, conv_sample = trunk(
        x_sample, [m[n_prompt:] for m in mods], P, cache_k, cache_v, state_conv, page_table)
    return (y_prompt, y_sample, k_prompt, v_prompt, k_sample, v_sample, sgu_v_sample, conv_prompt, conv_sample)
```

```python
import functools
import math

import jax
import jax.numpy as jnp
from jax import lax
from jax.experimental import pallas as pl
from jax.experimental.pallas import tpu as pltpu

D_MODEL = 4096
DEPTH = 4
PAGE_SIZE = 128
DH = 128
D_ATT = D_MODEL // 2
H_A = D_ATT // (2 * DH)
ATTN_SCALE = DH ** -0.5
Q_BLOCK = 128
N_BUCKETS = 32
MAX_DISTANCE = 128
D_SGU = D_MODEL - D_ATT
SGU_GROUPS = 8
SGU_CG = D_SGU // SGU_GROUPS
CHUNK = 128
D_CONV = D_MODEL
CONV_W = 31
PEER_HEADS = 8
N_KEYS = 128
PEER_TOPK = 16
D_KEY = 256
PEER_BLOCK = 64
RMS_EPS = 1e-6
LN_EPS = 1e-5

VMEM_LIMIT_BYTES = 48 * 1024 * 1024


def _mm_kernel(x_ref, w_ref, b_ref, o_ref, acc_ref):
    k = pl.program_id(2)

    @pl.when(k == 0)
    def _():
        acc_ref[...] = jnp.zeros_like(acc_ref)

    x, w = x_ref[...], w_ref[...]
    if x.dtype != jnp.bfloat16:
        x = x.astype(jnp.bfloat16)
    if w.dtype != jnp.bfloat16:
        w = w.astype(jnp.bfloat16)
    acc_ref[...] += jnp.dot(x, w, preferred_element_type=jnp.float32)

    @pl.when(k == pl.num_programs(2) - 1)
    def _():
        o_ref[...] = acc_ref[...] + b_ref[...]


def _pick(dim, pref):
    for c in pref:
        if dim % c == 0:
            return c
    return dim


def mm(x, w, b=None, layer=None):
    m, kdim = x.shape
    n = w.shape[-1]
    if layer is None:
        w_block, w_index = (lambda bk, bn: (bk, bn)), (lambda i, j, k: (k, j))
    else:
        w_block, w_index = (lambda bk, bn: (None, bk, bn)), (lambda i, j, k: (layer, k, j))
    x = x.astype(jnp.bfloat16)
    mp = -(-m // 16) * 16
    if mp != m:
        x = jnp.pad(x, ((0, mp - m), (0, 0)))
    if b is None:
        b = jnp.zeros((n,), jnp.float32)
    bm = _pick(mp, (2048, 1024, 512, 256, 128, 64, 32, 16))
    bn = _pick(n, (1024, 512, 256, 128))
    bk = _pick(kdim, (1024, 512, 256, 128))
    out = pl.pallas_call(
        _mm_kernel,
        grid=(mp // bm, n // bn, kdim // bk),
        in_specs=[pl.BlockSpec((bm, bk), lambda i, j, k: (i, k)),
                  pl.BlockSpec(w_block(bk, bn), w_index),
                  pl.BlockSpec((1, bn), lambda i, j, k: (0, j))],
        out_specs=pl.BlockSpec((bm, bn), lambda i, j, k: (i, j)),
        out_shape=jax.ShapeDtypeStruct((mp, n), jnp.float32),
        scratch_shapes=[pltpu.VMEM((bm, bn), jnp.float32)],
        compiler_params=pltpu.CompilerParams(
            dimension_semantics=("parallel", "parallel", "arbitrary"),
            vmem_limit_bytes=VMEM_LIMIT_BYTES),
        name="mm",
    )(x, w, b.reshape(1, n))
    return out[:m]


def mm3(x, w, b=None, layer=None):
    bsz, t, d = x.shape
    return mm(x.reshape(bsz * t, d), w, b, layer).reshape(bsz, t, w.shape[-1])


def _top16_cols(s, n_rows):
    rows = lax.broadcasted_iota(jnp.int32, s.shape, 0).astype(jnp.float32)
    vals, ids = [], []
    for _ in range(PEER_TOPK):
        m = jnp.max(s, axis=0, keepdims=True)
        idx = jnp.min(jnp.where(s == m, rows, float(n_rows)), axis=0, keepdims=True)
        s = jnp.where(rows == idx, -jnp.inf, s)
        vals.append(m)
        ids.append(idx)
    return jnp.concatenate(vals, axis=0), jnp.concatenate(ids, axis=0)


def _route_kernel(q_ref, sk_ref, eid_ref, gate_ref):
    half = D_KEY // 2
    nt = (((1,), (1,)), ((), ()))
    sv, si = [], []
    for c in range(2):
        qc = q_ref[:, c * half:(c + 1) * half].astype(jnp.bfloat16)
        s = lax.dot_general(sk_ref[c].astype(jnp.bfloat16), qc, nt,
                            preferred_element_type=jnp.float32)
        v, i = _top16_cols(s, N_KEYS)
        sv.append(v)
        si.append(i)
    t_lanes = sv[0].shape[1]
    row8 = lax.broadcasted_iota(jnp.int32, (SUBLANES, t_lanes), 0)

    def staircase(a0, a1, combine, fill):
        def bc(x, i):
            return jnp.broadcast_to(x[i:i + 1, :], (SUBLANES, t_lanes))
        lo8 = a1[0:SUBLANES]
        a56 = jnp.where(row8 < 2, bc(a0, 5), jnp.where(row8 < 4, bc(a0, 6), bc(a0, 7)))
        b56 = jnp.where(row8 % 2 == 0, bc(a1, 0), bc(a1, 1))
        return jnp.concatenate([
            combine(bc(a0, 0), lo8),
            combine(bc(a0, 0), a1[SUBLANES:2 * SUBLANES]),
            combine(bc(a0, 1), lo8),
            jnp.where(row8 < 5, combine(bc(a0, 2), lo8), fill),
            jnp.where(row8 < 4, combine(bc(a0, 3), lo8), fill),
            jnp.where(row8 < 3, combine(bc(a0, 4), lo8), fill),
            jnp.where(row8 < 6, combine(a56, b56), fill),
            combine(a0[SUBLANES:2 * SUBLANES], bc(a1, 0)),
        ], axis=0)

    cand = staircase(sv[0], sv[1], lambda a, b: a + b, -jnp.inf)
    cid = staircase(si[0], si[1], lambda a, b: a * float(N_KEYS) + b, 0.0)
    rows = lax.broadcasted_iota(jnp.int32, cand.shape, 0).astype(jnp.float32)
    n_cand = cand.shape[0]
    best, eids = [], []
    for _ in range(PEER_TOPK):
        m = jnp.max(cand, axis=0, keepdims=True)
        pos = jnp.min(jnp.where(cand == m, rows, float(n_cand)), axis=0, keepdims=True)
        hit = rows == pos
        eids.append(jnp.sum(jnp.where(hit, cid, 0.0), axis=0, keepdims=True))
        cand = jnp.where(hit, -jnp.inf, cand)
        best.append(m)
    best = jnp.concatenate(best, axis=0)
    e = jnp.exp(best - best[0:1, :])
    gate_ref[0] = e / jnp.sum(e, axis=0, keepdims=True)
    eid_ref[0] = jnp.concatenate(eids, axis=0).astype(jnp.int32)


def peer_route(q, sk):
    n = q.shape[0]
    tb = min(128, n)
    eid, gate = pl.pallas_call(
        _route_kernel,
        grid=(n // tb, PEER_HEADS),
        in_specs=[pl.BlockSpec((tb, D_KEY), lambda i, h: (i, h)),
                  pl.BlockSpec((2, N_KEYS, D_KEY // 2), lambda i, h: (0, 0, 0))],
        out_specs=[pl.BlockSpec((1, PEER_TOPK, tb), lambda i, h: (h, 0, i)),
                   pl.BlockSpec((1, PEER_TOPK, tb), lambda i, h: (h, 0, i))],
        out_shape=[jax.ShapeDtypeStruct((PEER_HEADS, PEER_TOPK, n), jnp.int32),
                   jax.ShapeDtypeStruct((PEER_HEADS, PEER_TOPK, n), jnp.float32)],
        compiler_params=pltpu.CompilerParams(dimension_semantics=("parallel", "parallel")),
        name="peer_route",
    )(q, sk)
    n_sel = PEER_HEADS * PEER_TOPK
    return eid.reshape(n_sel, n).T, gate.reshape(n_sel, n).T


PEER_GROUP = 8
PEER_RING = 2
N_SEL = PEER_HEADS * PEER_TOPK
SUBLANES = 8
LANES = 128


def _gelu_exact(x):
    return 0.5 * x * (1.0 + lax.erf(x * (2.0 ** -0.5)))


PACK_ROWS = 64


def _pack_kernel(u_ref, v_ref, o_ref):
    def bf16_bits(x):
        return lax.bitcast_convert_type(x.astype(jnp.bfloat16).astype(jnp.float32), jnp.uint32)
    lo = lax.shift_right_logical(bf16_bits(u_ref[...]), jnp.uint32(16))
    hi = bf16_bits(v_ref[...]) & jnp.uint32(0xFFFF0000)
    o_ref[:, 0, :] = hi | lo


def pack_experts(eu, ev, layer=None):
    e, d = eu.shape[-2:]
    if layer is None:
        spec = pl.BlockSpec((PACK_ROWS, d), lambda i: (i, 0))
    else:
        spec = pl.BlockSpec((None, PACK_ROWS, d), lambda i: (layer, i, 0))
    return pl.pallas_call(
        _pack_kernel,
        grid=(e // PACK_ROWS,),
        in_specs=[spec, spec],
        out_specs=pl.BlockSpec((PACK_ROWS, 1, d), lambda i: (i, 0, 0)),
        out_shape=jax.ShapeDtypeStruct((e, 1, d), jnp.uint32),
        compiler_params=pltpu.CompilerParams(dimension_semantics=("parallel",)),
        name="pack_experts",
    )(eu, ev)


def _split3(p):
    hi = p.astype(jnp.bfloat16)
    r1 = p - hi.astype(jnp.float32)
    mid = r1.astype(jnp.bfloat16)
    lo = (r1 - mid.astype(jnp.float32)).astype(jnp.bfloat16)
    return hi, mid, lo


def _peer_kernel(eid0, eid1, x_ref, gate_ref, w_hbm, out_ref, wbuf, part, acol, xbs, sem):
    g = pl.program_id(0)
    n_groups = pl.num_programs(0)
    s = lax.rem(g, PEER_RING)
    n_tiles = N_SEL // SUBLANES
    n_lane_tiles = D_MODEL // LANES
    nt = (((1,), (1,)), ((), ()))

    def row_copy(eid_ref, et, j, b, t):
        return pltpu.make_async_copy(w_hbm.at[eid_ref[0, et, j]], wbuf.at[b, t, pl.ds(j, 1), :], sem.at[b, t])

    def wait_slot(b, t):
        pltpu.make_async_copy(wbuf.at[b, t], wbuf.at[b, t], sem.at[b, t]).wait()

    @pl.when(g == 0)
    def _():
        def fill(eid_ref, b, n_tok):
            def body(t, carry):
                for j in range(N_SEL):
                    row_copy(eid_ref, t, j, b, t).start()
                return carry
            lax.fori_loop(0, n_tok, body, 0)
        fill(eid0, 0, PEER_GROUP)

    sub_iota = lax.broadcasted_iota(jnp.int32, (SUBLANES, LANES), 0)
    lane_iota = lax.broadcasted_iota(jnp.int32, (SUBLANES, LANES), 1)
    hi_mask = jnp.uint32(0xFFFF0000)
    half_d = D_MODEL // 2
    per_tile = N_SEL // (4 * n_tiles)

    def tree_sum(parts):
        while len(parts) > 1:
            parts = [a + b for a, b in zip(parts[0::2], parts[1::2])]
        return parts[0]

    def start_next(tile_step):
        for i in range(tile_step * per_tile, (tile_step + 1) * per_tile):
            row_copy(eid1, i // N_SEL, i % N_SEL, 1 - s, i // N_SEL).start()

    for t in range(PEER_GROUP):
        wait_slot(s, t)
        xbs[...] = jnp.broadcast_to(x_ref[t:t + 1, :], (SUBLANES, D_MODEL))
        for r in range(n_tiles):
            halves = []
            for hf in range(2):
                w = wbuf[s, t, r * SUBLANES:(r + 1) * SUBLANES, hf * half_d:(hf + 1) * half_d]
                xb = xbs[:, hf * half_d:(hf + 1) * half_d]
                start_next(2 * (t * n_tiles + r) + hf)
                prod = lax.bitcast_convert_type(w << 16, jnp.float32) * xb
                halves.append(tree_sum([prod[:, c * LANES:(c + 1) * LANES] for c in range(n_lane_tiles // 2)]))
            part[t, r * SUBLANES:(r + 1) * SUBLANES, :] = halves[0] + halves[1]

    ones = jnp.ones((SUBLANES, LANES), jnp.bfloat16)
    h_rows = jnp.zeros((PEER_GROUP, N_SEL), jnp.float32)
    for t in range(PEER_GROUP):
        h_t = sum(lax.dot_general(ones, piece, nt, preferred_element_type=jnp.float32)
                  for piece in _split3(part[t]))
        h_rows = jnp.where(lax.broadcasted_iota(jnp.int32, h_rows.shape, 0) == t, h_t, h_rows)
    a_rows = _gelu_exact(h_rows) * gate_ref[...]
    for t in range(PEER_GROUP):
        ab = jnp.broadcast_to(a_rows[t:t + 1, :], (SUBLANES, N_SEL))
        for r in range(n_tiles):
            col = jnp.sum(jnp.where(lane_iota == sub_iota + r * SUBLANES, ab, 0.0), axis=1, keepdims=True)
            acol[t, r * SUBLANES:(r + 1) * SUBLANES, :] = jnp.broadcast_to(col, (SUBLANES, LANES))

    rows = []
    for t in range(PEER_GROUP):
        row = []
        for hf in range(2):
            acc = jnp.zeros((SUBLANES, half_d), jnp.float32)
            for r in range(n_tiles):
                w = wbuf[s, t, r * SUBLANES:(r + 1) * SUBLANES, hf * half_d:(hf + 1) * half_d]
                a = acol[t, r * SUBLANES:(r + 1) * SUBLANES, :]
                start_next(2 * PEER_GROUP * n_tiles + (2 * t + hf) * n_tiles + r)
                acc = acc + (lax.bitcast_convert_type(w & hi_mask, jnp.float32)
                             * jnp.concatenate([a] * (n_lane_tiles // 2), axis=1))
            row.append(jnp.sum(acc, axis=0, keepdims=True))
        rows.append(jnp.concatenate(row, axis=1))
    out_ref[...] = jnp.concatenate(rows, axis=0)

    @pl.when(g == n_groups - 1)
    def _():
        for t in range(PEER_GROUP):
            wait_slot(1 - s, t)


def peer_experts(x, eid, gate, w_packed):
    n = x.shape[0]
    n_groups = n // PEER_GROUP
    assert n % PEER_GROUP == 0 and n_groups >= PEER_RING
    eid3 = eid.reshape(n_groups, PEER_GROUP, N_SEL)
    smem_block = functools.partial(pl.BlockSpec, (1, PEER_GROUP, N_SEL), memory_space=pltpu.SMEM)
    return pl.pallas_call(
        _peer_kernel,
        grid=(n_groups,),
        in_specs=[smem_block(lambda g: (g, 0, 0)),
                  smem_block(lambda g: (jnp.minimum(g + 1, n_groups - 1), 0, 0)),
                  pl.BlockSpec((PEER_GROUP, D_MODEL), lambda g: (g, 0)),
                  pl.BlockSpec((PEER_GROUP, N_SEL), lambda g: (g, 0)),
                  pl.BlockSpec(memory_space=pl.ANY)],
        out_specs=pl.BlockSpec((PEER_GROUP, D_MODEL), lambda g: (g, 0)),
        out_shape=jax.ShapeDtypeStruct((n, D_MODEL), jnp.float32),
        scratch_shapes=[pltpu.VMEM((PEER_RING, PEER_GROUP, N_SEL, D_MODEL), jnp.uint32),
                        pltpu.VMEM((PEER_GROUP, N_SEL, LANES), jnp.float32),
                        pltpu.VMEM((PEER_GROUP, N_SEL, LANES), jnp.float32),
                        pltpu.VMEM((SUBLANES, D_MODEL), jnp.float32),
                        pltpu.SemaphoreType.DMA((PEER_RING, PEER_GROUP))],
        compiler_params=pltpu.CompilerParams(dimension_semantics=("arbitrary",),
                                             vmem_limit_bytes=VMEM_LIMIT_BYTES),
        name="peer_experts",
    )(eid3, eid3, x, gate, w_packed)


ATT_BLOCK = 256
FAR_BUCKET_MIN_DIST = MAX_DISTANCE


def _attn_kernel(lam_ref, bfar_ref, q_ref, k_ref, v_ref, bd_ref, bs_ref, g_ref, o_ref, m_ref, l_ref, acc_ref,
                 *, out_scale):
    h = pl.program_id(1)
    i = pl.program_id(2)
    nt = (((1,), (1,)), ((), ()))
    m_ref[...] = jnp.full(m_ref.shape, -jnp.inf, jnp.float32)
    l_ref[...] = jnp.zeros(l_ref.shape, jnp.float32)
    acc_ref[...] = jnp.zeros(acc_ref.shape, jnp.float32)

    def step(start, bias):
        kb = k_ref[pl.ds(start, ATT_BLOCK), :]
        vb = v_ref[pl.ds(start, ATT_BLOCK), :]
        for c in range(2):
            s = lax.dot_general(q_ref[:, c * DH:(c + 1) * DH], kb[:, c * DH:(c + 1) * DH], nt,
                                preferred_element_type=jnp.float32) * ATTN_SCALE + bias
            m_old = m_ref[c]
            m_new = jnp.maximum(m_old, jnp.max(s, axis=1, keepdims=True))
            alpha = jnp.exp(m_old - m_new)
            p = jnp.exp(s - m_new[:, 0:1])
            l_ref[c] = alpha * l_ref[c] + jnp.sum(p, axis=1, keepdims=True)
            acc_ref[c] = acc_ref[c] * alpha[:, 0:1] + jnp.dot(p.astype(jnp.bfloat16), vb,
                                                             preferred_element_type=jnp.float32)
            m_ref[c] = m_new

    def far_body(j, carry):
        step(pl.multiple_of(j * ATT_BLOCK, ATT_BLOCK), bfar_ref[h])
        return carry

    lax.fori_loop(0, jnp.maximum(i - 1, 0), far_body, 0)

    @pl.when(i >= 1)
    def _():
        step(pl.multiple_of((i - 1) * ATT_BLOCK, ATT_BLOCK), bs_ref[0])

    step(pl.multiple_of(i * ATT_BLOCK, ATT_BLOCK), bd_ref[0])

    o = acc_ref[0] / l_ref[0][:, 0:1] - lam_ref[0] * (acc_ref[1] / l_ref[1][:, 0:1])
    y = o * lax.rsqrt(jnp.mean(o * o, axis=1, keepdims=True) + RMS_EPS)
    o_ref[...] = y * g_ref[...] * out_scale


def prompt_attention(q, k, v, lam, rel_bias, subln_gain, out_scale, seq):
    n, width = q.shape
    n_heads = width // (2 * DH)
    assert ATT_BLOCK >= FAR_BUCKET_MIN_DIST and seq % ATT_BLOCK == 0
    r = jnp.arange(ATT_BLOCK, dtype=jnp.int32)
    d_diag = r[:, None] - r[None, :]
    bias_diag = jnp.where(d_diag >= 0, jnp.moveaxis(rel_bias[rel_bucket(d_diag)], -1, 0), -jnp.inf)
    bias_sub = jnp.moveaxis(rel_bias[rel_bucket(d_diag + ATT_BLOCK)], -1, 0)
    bias_far = rel_bias[N_BUCKETS - 1]
    nq = seq // ATT_BLOCK
    smem = pl.BlockSpec(memory_space=pltpu.SMEM)
    return pl.pallas_call(
        functools.partial(_attn_kernel, out_scale=out_scale),
        grid=(n // seq, n_heads, nq),
        in_specs=[smem, smem,
                  pl.BlockSpec((ATT_BLOCK, 2 * DH), lambda b, h, i: (b * nq + i, h)),
                  pl.BlockSpec((seq, 2 * DH), lambda b, h, i: (b, h)),
                  pl.BlockSpec((seq, 2 * DH), lambda b, h, i: (b, h)),
                  pl.BlockSpec((1, ATT_BLOCK, ATT_BLOCK), lambda b, h, i: (h, 0, 0)),
                  pl.BlockSpec((1, ATT_BLOCK, ATT_BLOCK), lambda b, h, i: (h, 0, 0)),
                  pl.BlockSpec((1, 2 * DH), lambda b, h, i: (0, 0))],
        out_specs=pl.BlockSpec((ATT_BLOCK, 2 * DH), lambda b, h, i: (b * nq + i, h)),
        out_shape=jax.ShapeDtypeStruct((n, width), jnp.float32),
        scratch_shapes=[pltpu.VMEM((2, ATT_BLOCK, LANES), jnp.float32),
                        pltpu.VMEM((2, ATT_BLOCK, LANES), jnp.float32),
                        pltpu.VMEM((2, ATT_BLOCK, 2 * DH), jnp.float32)],
        compiler_params=pltpu.CompilerParams(dimension_semantics=("parallel", "parallel", "parallel"),
                                             vmem_limit_bytes=VMEM_LIMIT_BYTES),
        name="prompt_attention",
    )(lam.reshape(1), bias_far, q, k, v, bias_diag, bias_sub, subln_gain.reshape(1, 2 * DH))


def rmsnorm(x, g):
    y = x * lax.rsqrt(jnp.mean(x * x, axis=-1, keepdims=True) + RMS_EPS)
    return y * g


def layernorm(x, g, b):
    mu = jnp.mean(x, axis=-1, keepdims=True)
    xc = x - mu
    var = jnp.mean(xc * xc, axis=-1, keepdims=True)
    return xc * lax.rsqrt(var + LN_EPS) * g + b


def rel_bucket(dist):
    n = jnp.maximum(dist, 0)
    max_exact = N_BUCKETS // 2
    nf = jnp.maximum(n, 1).astype(jnp.float32)
    large = max_exact + (jnp.log(nf / max_exact) / math.log(MAX_DISTANCE / max_exact)
                         * (N_BUCKETS - max_exact)).astype(jnp.int32)
    large = jnp.minimum(large, N_BUCKETS - 1)
    return jnp.where(n < max_exact, n, large)


def diff_attend(q, k, v, q_pos, k_pos, lam, rel_bias):
    dist = q_pos[:, None] - k_pos[None, :]
    bias = jnp.moveaxis(rel_bias[rel_bucket(dist)], -1, 0).astype(jnp.float32)
    s = jnp.einsum('bqhcd,bkhcd->bhcqk', q, k).astype(jnp.float32) * ATTN_SCALE + bias[None, :, None]
    s = jnp.where(dist >= 0, s, -jnp.inf)
    p = jax.nn.softmax(s, axis=-1)
    w = p[:, :, 0] - lam * p[:, :, 1]
    return jnp.einsum('bhqk,bkhe->bqhe', w.astype(v.dtype), v)


def chunk_mix(g, w_s, b_s):
    B, T, G, C = g.shape
    pad = (-T) % CHUNK
    gc = jnp.pad(g, ((0, 0), (0, pad), (0, 0), (0, 0))).reshape(B, (T + pad) // CHUNK, CHUNK, G, C)
    w = jnp.where(jnp.tril(jnp.ones((CHUNK, CHUNK), dtype=bool)), w_s, 0.0)
    m = jnp.einsum('gts,bnsgc->bntgc', w.astype(g.dtype), gc) + b_s.T[None, None, :, :, None].astype(g.dtype)
    return m.reshape(B, T + pad, G, C)[:, :T]


def even_mixer(xm, i, layer_idx, P, past_k, past_v):
    B, T, _ = xm.shape
    f32 = jnp.float32
    proj = mm3(xm, P['w_in'], layer=i)
    q, k, v, u, g = jnp.split(proj, [D_ATT, 2 * D_ATT, 3 * D_ATT, 3 * D_ATT + D_SGU], axis=-1)
    q = rmsnorm(q.reshape(B, T, H_A, 2, DH), P['q_gain'][i])
    k = rmsnorm(k.reshape(B, T, H_A, 2, DH), P['k_gain'][i])
    v = v.reshape(B, T, H_A, 2 * DH)
    lam_init = 0.8 - 0.6 * math.exp(-0.3 * layer_idx)
    lam = (jnp.exp(jnp.sum(P['lam_q1'][i].astype(f32) * P['lam_k1'][i].astype(f32)))
           - jnp.exp(jnp.sum(P['lam_q2'][i].astype(f32) * P['lam_k2'][i].astype(f32))) + lam_init)
    if past_k is None:
        bf16 = jnp.bfloat16
        attn = prompt_attention(q.reshape(B * T, D_ATT).astype(bf16), k.reshape(B * T, D_ATT).astype(bf16),
                                v.reshape(B * T, D_ATT).astype(bf16), lam, P['rel_bias'], P['subln_gain'][i],
                                1.0 - lam_init, T).reshape(B, T, D_ATT)
    else:
        past = past_k.shape[1]
        k_all = jnp.concatenate([past_k.astype(k.dtype), k], axis=1)
        v_all = jnp.concatenate([past_v.astype(v.dtype), v], axis=1)
        o = diff_attend(q, k_all, v_all, past + jnp.arange(T, dtype=jnp.int32),
                        jnp.arange(past + T, dtype=jnp.int32), lam, P['rel_bias'])
        attn = (rmsnorm(o, P['subln_gain'][i]) * (1.0 - lam_init)).reshape(B, T, D_ATT)
    u = jax.nn.gelu(u, approximate=False)
    g = layernorm(jax.nn.gelu(g, approximate=False), P['sgu_ln_gain'][i], P['sgu_ln_bias'][i])
    s_out = u * chunk_mix(g.reshape(B, T, SGU_GROUPS, SGU_CG), P['sgu_w'][i], P['sgu_b'][i]).reshape(B, T, D_SGU)
    y = mm3(jnp.concatenate([attn, s_out], axis=-1), P['w_out'], layer=i)
    return y, k.reshape(B, T, H_A, 2 * DH), v, g


def conv_mixer(xm, i, P, buf):
    a = mm3(xm, P['w_pw1'], P['b_pw1'][i], layer=i)
    val, gate = jnp.split(a, 2, axis=-1)
    glu = val * jax.nn.sigmoid(gate)
    full = jnp.concatenate([buf.astype(glu.dtype), glu], axis=1)
    h = lax.conv_general_dilated(full, P['w_dw'][i][:, None, :].astype(full.dtype), (1,), 'VALID',
                                 dimension_numbers=('NWC', 'WIO', 'NWC'),
                                 feature_group_count=D_CONV) + P['b_dw'][i]
    h = jax.nn.silu(layernorm(h, P['conv_ln_gain'][i], P['conv_ln_bias'][i]))
    return mm3(h, P['w_pw2'], P['b_pw2'][i], layer=i), full[:, -(CONV_W - 1):]


def peer(xm, l, P):
    B, T, D = xm.shape
    n = B * T
    xf = xm.reshape(n, D)
    eid, gate = peer_route(mm(xf, P['w_query'], layer=l), P['sub_keys'][l])
    return peer_experts(xf, eid, gate, P['experts'][l]).reshape(B, T, D)


def trunk(x, mods, P, cache_k, cache_v, conv_state, page_table):
    B = x.shape[0]
    sample = page_table is not None
    new_k, new_v, new_g, new_conv = [], [], [], []
    for l in range(DEPTH):
        sh1, sc1, gt1, sh2, sc2, gt2 = jnp.split(mods[l][:, None, :], 6, axis=-1)
        xm = rmsnorm(x, P['norm_gain'][l, 0]) * (1 + sc1) + sh1
        i = l // 2
        if l % 2 == 0:
            if sample:
                past = page_table.shape[1] * PAGE_SIZE
                pk = cache_k[i][page_table].reshape(B, past, H_A, 2, DH)
                pv = cache_v[i][page_table].reshape(B, past, H_A, 2 * DH)
            else:
                pk, pv = None, None
            mix, kr, vr, gr = even_mixer(xm, i, l, P, pk, pv)
            new_k.append(kr)
            new_v.append(vr)
            if sample:
                new_g.append(gr)
        else:
            buf = conv_state[i] if sample else jnp.zeros((B, CONV_W - 1, D_CONV), x.dtype)
            mix, cb = conv_mixer(xm, i, P, buf)
            new_conv.append(cb)
        x = x + gt1 * mix
        hm = rmsnorm(x, P['norm_gain'][l, 1]) * (1 + sc2) + sh2
        x = x + gt2 * peer(hm, l, P)
    g_rows = jnp.stack(new_g) if sample else None
    return x, jnp.stack(new_k), jnp.stack(new_v), g_rows, jnp.stack(new_conv)


def kernel(x_prompt, x_sample, c_prompt, c_sample, cache_k, cache_v, state_conv, page_table, norm_gain, w_ada, b_ada, rel_bias, w_in, q_gain, k_gain, lam_q1, lam_k1, lam_q2, lam_k2, subln_gain, sgu_ln_gain, sgu_ln_bias, sgu_w, sgu_b, w_out, w_pw1, b_pw1, w_dw, b_dw, conv_ln_gain, conv_ln_bias, w_pw2, b_pw2, w_query, sub_keys, expert_u, expert_v):
    bf16 = jnp.bfloat16
    P = dict(norm_gain=norm_gain, rel_bias=rel_bias, w_in=w_in.astype(bf16),
             q_gain=q_gain, k_gain=k_gain, lam_q1=lam_q1, lam_k1=lam_k1, lam_q2=lam_q2, lam_k2=lam_k2,
             subln_gain=subln_gain, sgu_ln_gain=sgu_ln_gain, sgu_ln_bias=sgu_ln_bias, sgu_w=sgu_w,
             sgu_b=sgu_b, w_out=w_out.astype(bf16), w_pw1=w_pw1.astype(bf16), b_pw1=b_pw1, w_dw=w_dw, b_dw=b_dw,
             conv_ln_gain=conv_ln_gain, conv_ln_bias=conv_ln_bias, w_pw2=w_pw2.astype(bf16), b_pw2=b_pw2,
             w_query=w_query.astype(bf16), sub_keys=sub_keys,
             experts=[pack_experts(expert_u, expert_v, layer=l) for l in range(DEPTH)])
    n_prompt = c_prompt.shape[0]
    cs = jax.nn.silu(jnp.concatenate([c_prompt, c_sample], axis=0))
    mods = [mm(cs, w_ada, b_ada[l], layer=l) for l in range(DEPTH)]
    y_prompt, k_prompt, v_prompt, _, conv_prompt = trunk(
        x_prompt, [m[:n_prompt] for m in mods], P, None, None, None, None)
    y_sample, k_sample, v_sample, sgu_v_sample, conv_sample = trunk(
        x_sample, [m[n_prompt:] for m in mods], P, cache_k, cache_v, state_conv, page_table)
    return (y_prompt, y_sample, k_prompt, v_prompt, k_sample, v_sample, sgu_v_sample, conv_prompt, conv_sample)
```

```python
import functools
import math

import jax
import jax.numpy as jnp
from jax import lax
from jax.experimental import pallas as pl
from jax.experimental.pallas import tpu as pltpu

D_MODEL = 4096
DEPTH = 4
PAGE_SIZE = 128
DH = 128
D_ATT = D_MODEL // 2
H_A = D_ATT // (2 * DH)
ATTN_SCALE = DH ** -0.5
Q_BLOCK = 128
N_BUCKETS = 32
MAX_DISTANCE = 128
D_SGU = D_MODEL - D_ATT
SGU_GROUPS = 8
SGU_CG = D_SGU // SGU_GROUPS
CHUNK = 128
D_CONV = D_MODEL
CONV_W = 31
PEER_HEADS = 8
N_KEYS = 128
PEER_TOPK = 16
D_KEY = 256
PEER_BLOCK = 64
RMS_EPS = 1e-6
LN_EPS = 1e-5

VMEM_LIMIT_BYTES = 48 * 1024 * 1024


def _mm_kernel(x_ref, w_ref, b_ref, o_ref, acc_ref):
    k = pl.program_id(2)

    @pl.when(k == 0)
    def _():
        acc_ref[...] = jnp.zeros_like(acc_ref)

    x, w = x_ref[...], w_ref[...]
    if x.dtype != jnp.bfloat16:
        x = x.astype(jnp.bfloat16)
    if w.dtype != jnp.bfloat16:
        w = w.astype(jnp.bfloat16)
    acc_ref[...] += jnp.dot(x, w, preferred_element_type=jnp.float32)

    @pl.when(k == pl.num_programs(2) - 1)
    def _():
        o_ref[...] = acc_ref[...] + b_ref[...]


def _pick(dim, pref):
    for c in pref:
        if dim % c == 0:
            return c
    return dim


def mm(x, w, b=None, layer=None):
    m, kdim = x.shape
    n = w.shape[-1]
    if layer is None:
        w_block, w_index = (lambda bk, bn: (bk, bn)), (lambda i, j, k: (k, j))
    else:
        w_block, w_index = (lambda bk, bn: (None, bk, bn)), (lambda i, j, k: (layer, k, j))
    x = x.astype(jnp.bfloat16)
    mp = -(-m // 16) * 16
    if mp != m:
        x = jnp.pad(x, ((0, mp - m), (0, 0)))
    if b is None:
        b = jnp.zeros((n,), jnp.float32)
    bm = _pick(mp, (2048, 1024, 512, 256, 128, 64, 32, 16))
    bn = _pick(n, (1024, 512, 256, 128))
    bk = _pick(kdim, (1024, 512, 256, 128))
    out = pl.pallas_call(
        _mm_kernel,
        grid=(mp // bm, n // bn, kdim // bk),
        in_specs=[pl.BlockSpec((bm, bk), lambda i, j, k: (i, k)),
                  pl.BlockSpec(w_block(bk, bn), w_index),
                  pl.BlockSpec((1, bn), lambda i, j, k: (0, j))],
        out_specs=pl.BlockSpec((bm, bn), lambda i, j, k: (i, j)),
        out_shape=jax.ShapeDtypeStruct((mp, n), jnp.float32),
        scratch_shapes=[pltpu.VMEM((bm, bn), jnp.float32)],
        compiler_params=pltpu.CompilerParams(
            dimension_semantics=("parallel", "parallel", "arbitrary"),
            vmem_limit_bytes=VMEM_LIMIT_BYTES),
        name="mm",
    )(x, w, b.reshape(1, n))
    return out[:m]


def mm3(x, w, b=None, layer=None):
    bsz, t, d = x.shape
    return mm(x.reshape(bsz * t, d), w, b, layer).reshape(bsz, t, w.shape[-1])


def _top16_cols(s, n_rows):
    rows = lax.broadcasted_iota(jnp.int32, s.shape, 0).astype(jnp.float32)
    vals, ids = [], []
    for _ in range(PEER_TOPK):
        m = jnp.max(s, axis=0, keepdims=True)
        idx = jnp.min(jnp.where(s == m, rows, float(n_rows)), axis=0, keepdims=True)
        s = jnp.where(rows == idx, -jnp.inf, s)
        vals.append(m)
        ids.append(idx)
    return jnp.concatenate(vals, axis=0), jnp.concatenate(ids, axis=0)


def _route_kernel(q_ref, sk_ref, eid_ref, gate_ref):
    half = D_KEY // 2
    nt = (((1,), (1,)), ((), ()))
    sv, si = [], []
    for c in range(2):
        qc = q_ref[:, c * half:(c + 1) * half].astype(jnp.bfloat16)
        s = lax.dot_general(sk_ref[c].astype(jnp.bfloat16), qc, nt,
                            preferred_element_type=jnp.float32)
        v, i = _top16_cols(s, N_KEYS)
        sv.append(v)
        si.append(i)
    t_lanes = sv[0].shape[1]
    row8 = lax.broadcasted_iota(jnp.int32, (SUBLANES, t_lanes), 0)

    def staircase(a0, a1, combine, fill):
        def bc(x, i):
            return jnp.broadcast_to(x[i:i + 1, :], (SUBLANES, t_lanes))
        lo8 = a1[0:SUBLANES]
        a56 = jnp.where(row8 < 2, bc(a0, 5), jnp.where(row8 < 4, bc(a0, 6), bc(a0, 7)))
        b56 = jnp.where(row8 % 2 == 0, bc(a1, 0), bc(a1, 1))
        return jnp.concatenate([
            combine(bc(a0, 0), lo8),
            combine(bc(a0, 0), a1[SUBLANES:2 * SUBLANES]),
            combine(bc(a0, 1), lo8),
            jnp.where(row8 < 5, combine(bc(a0, 2), lo8), fill),
            jnp.where(row8 < 4, combine(bc(a0, 3), lo8), fill),
            jnp.where(row8 < 3, combine(bc(a0, 4), lo8), fill),
            jnp.where(row8 < 6, combine(a56, b56), fill),
            combine(a0[SUBLANES:2 * SUBLANES], bc(a1, 0)),
        ], axis=0)

    cand = staircase(sv[0], sv[1], lambda a, b: a + b, -jnp.inf)
    cid = staircase(si[0], si[1], lambda a, b: a * float(N_KEYS) + b, 0.0)
    rows = lax.broadcasted_iota(jnp.int32, cand.shape, 0).astype(jnp.float32)
    n_cand = cand.shape[0]
    best, eids = [], []
    for _ in range(PEER_TOPK):
        m = jnp.max(cand, axis=0, keepdims=True)
        pos = jnp.min(jnp.where(cand == m, rows, float(n_cand)), axis=0, keepdims=True)
        hit = rows == pos
        eids.append(jnp.sum(jnp.where(hit, cid, 0.0), axis=0, keepdims=True))
        cand = jnp.where(hit, -jnp.inf, cand)
        best.append(m)
    best = jnp.concatenate(best, axis=0)
    e = jnp.exp(best - best[0:1, :])
    gate_ref[0] = e / jnp.sum(e, axis=0, keepdims=True)
    eid_ref[0] = jnp.concatenate(eids, axis=0).astype(jnp.int32)


def peer_route(q, sk):
    n = q.shape[0]
    tb = min(128, n)
    eid, gate = pl.pallas_call(
        _route_kernel,
        grid=(n // tb, PEER_HEADS),
        in_specs=[pl.BlockSpec((tb, D_KEY), lambda i, h: (i, h)),
                  pl.BlockSpec((2, N_KEYS, D_KEY // 2), lambda i, h: (0, 0, 0))],
        out_specs=[pl.BlockSpec((1, PEER_TOPK, tb), lambda i, h: (h, 0, i)),
                   pl.BlockSpec((1, PEER_TOPK, tb), lambda i, h: (h, 0, i))],
        out_shape=[jax.ShapeDtypeStruct((PEER_HEADS, PEER_TOPK, n), jnp.int32),
                   jax.ShapeDtypeStruct((PEER_HEADS, PEER_TOPK, n), jnp.float32)],
        compiler_params=pltpu.CompilerParams(dimension_semantics=("parallel", "parallel")),
        name="peer_route",
    )(q, sk)
    n_sel = PEER_HEADS * PEER_TOPK
    return eid.reshape(n_sel, n).T, gate.reshape(n_sel, n).T


PEER_GROUP = 8
PEER_RING = 2
N_SEL = PEER_HEADS * PEER_TOPK
SUBLANES = 8
LANES = 128


def _gelu_exact(x):
    return 0.5 * x * (1.0 + lax.erf(x * (2.0 ** -0.5)))


PACK_ROWS = 64


def _pack_kernel(u_ref, v_ref, o_ref):
    def bf16_bits(x):
        return lax.bitcast_convert_type(x.astype(jnp.bfloat16).astype(jnp.float32), jnp.uint32)
    lo = lax.shift_right_logical(bf16_bits(u_ref[...]), jnp.uint32(16))
    hi = bf16_bits(v_ref[...]) & jnp.uint32(0xFFFF0000)
    o_ref[:, 0, :] = hi | lo


def pack_experts(eu, ev, layer=None):
    e, d = eu.shape[-2:]
    if layer is None:
        spec = pl.BlockSpec((PACK_ROWS, d), lambda i: (i, 0))
    else:
        spec = pl.BlockSpec((None, PACK_ROWS, d), lambda i: (layer, i, 0))
    return pl.pallas_call(
        _pack_kernel,
        grid=(e // PACK_ROWS,),
        in_specs=[spec, spec],
        out_specs=pl.BlockSpec((PACK_ROWS, 1, d), lambda i: (i, 0, 0)),
        out_shape=jax.ShapeDtypeStruct((e, 1, d), jnp.uint32),
        compiler_params=pltpu.CompilerParams(dimension_semantics=("parallel",)),
        name="pack_experts",
    )(eu, ev)


def _split3(p):
    hi = p.astype(jnp.bfloat16)
    r1 = p - hi.astype(jnp.float32)
    mid = r1.astype(jnp.bfloat16)
    lo = (r1 - mid.astype(jnp.float32)).astype(jnp.bfloat16)
    return hi, mid, lo


def _peer_kernel(eid0, eid1, x_ref, gate_ref, w_hbm, out_ref, wbuf, part, acol, xbs, sem):
    g = pl.program_id(0)
    n_groups = pl.num_programs(0)
    s = lax.rem(g, PEER_RING)
    n_tiles = N_SEL // SUBLANES
    n_lane_tiles = D_MODEL // LANES
    nt = (((1,), (1,)), ((), ()))

    def row_copy(eid_ref, et, j, b, t):
        return pltpu.make_async_copy(w_hbm.at[eid_ref[0, et, j]], wbuf.at[b, t, pl.ds(j, 1), :], sem.at[b, t])

    def wait_slot(b, t):
        pltpu.make_async_copy(wbuf.at[b, t], wbuf.at[b, t], sem.at[b, t]).wait()

    @pl.when(g == 0)
    def _():
        def fill(eid_ref, b, n_tok):
            def body(t, carry):
                for j in range(N_SEL):
                    row_copy(eid_ref, t, j, b, t).start()
                return carry
            lax.fori_loop(0, n_tok, body, 0)
        fill(eid0, 0, PEER_GROUP)

    sub_iota = lax.broadcasted_iota(jnp.int32, (SUBLANES, LANES), 0)
    lane_iota = lax.broadcasted_iota(jnp.int32, (SUBLANES, LANES), 1)
    hi_mask = jnp.uint32(0xFFFF0000)
    half_d = D_MODEL // 2
    per_tile = N_SEL // (4 * n_tiles)

    def tree_sum(parts):
        while len(parts) > 1:
            parts = [a + b for a, b in zip(parts[0::2], parts[1::2])]
        return parts[0]

    def start_next(tile_step):
        for i in range(tile_step * per_tile, (tile_step + 1) * per_tile):
            row_copy(eid1, i // N_SEL, i % N_SEL, 1 - s, i // N_SEL).start(priority=i % 2)

    for t in range(PEER_GROUP):
        wait_slot(s, t)
        xbs[...] = jnp.broadcast_to(x_ref[t:t + 1, :], (SUBLANES, D_MODEL))
        for r in range(n_tiles):
            halves = []
            for hf in range(2):
                w = wbuf[s, t, r * SUBLANES:(r + 1) * SUBLANES, hf * half_d:(hf + 1) * half_d]
                xb = xbs[:, hf * half_d:(hf + 1) * half_d]
                start_next(2 * (t * n_tiles + r) + hf)
                prod = lax.bitcast_convert_type(w << 16, jnp.float32) * xb
                halves.append(tree_sum([prod[:, c * LANES:(c + 1) * LANES] for c in range(n_lane_tiles // 2)]))
            part[t, r * SUBLANES:(r + 1) * SUBLANES, :] = halves[0] + halves[1]

    ones = jnp.ones((SUBLANES, LANES), jnp.bfloat16)
    h_rows = jnp.zeros((PEER_GROUP, N_SEL), jnp.float32)
    for t in range(PEER_GROUP):
        h_t = sum(lax.dot_general(ones, piece, nt, preferred_element_type=jnp.float32)
                  for piece in _split3(part[t]))
        h_rows = jnp.where(lax.broadcasted_iota(jnp.int32, h_rows.shape, 0) == t, h_t, h_rows)
    a_rows = _gelu_exact(h_rows) * gate_ref[...]
    for t in range(PEER_GROUP):
        ab = jnp.broadcast_to(a_rows[t:t + 1, :], (SUBLANES, N_SEL))
        for r in range(n_tiles):
            col = jnp.sum(jnp.where(lane_iota == sub_iota + r * SUBLANES, ab, 0.0), axis=1, keepdims=True)
            acol[t, r * SUBLANES:(r + 1) * SUBLANES, :] = jnp.broadcast_to(col, (SUBLANES, LANES))

    rows = []
    for t in range(PEER_GROUP):
        row = []
        for hf in range(2):
            acc = jnp.zeros((SUBLANES, half_d), jnp.float32)
            for r in range(n_tiles):
                w = wbuf[s, t, r * SUBLANES:(r + 1) * SUBLANES, hf * half_d:(hf + 1) * half_d]
                a = acol[t, r * SUBLANES:(r + 1) * SUBLANES, :]
                start_next(2 * PEER_GROUP * n_tiles + (2 * t + hf) * n_tiles + r)
                acc = acc + (lax.bitcast_convert_type(w & hi_mask, jnp.float32)
                             * jnp.concatenate([a] * (n_lane_tiles // 2), axis=1))
            row.append(jnp.sum(acc, axis=0, keepdims=True))
        rows.append(jnp.concatenate(row, axis=1))
    out_ref[...] = jnp.concatenate(rows, axis=0)

    @pl.when(g == n_groups - 1)
    def _():
        for t in range(PEER_GROUP):
            wait_slot(1 - s, t)


def peer_experts(x, eid, gate, w_packed):
    n = x.shape[0]
    n_groups = n // PEER_GROUP
    assert n % PEER_GROUP == 0 and n_groups >= PEER_RING
    eid3 = eid.reshape(n_groups, PEER_GROUP, N_SEL)
    smem_block = functools.partial(pl.BlockSpec, (1, PEER_GROUP, N_SEL), memory_space=pltpu.SMEM)
    return pl.pallas_call(
        _peer_kernel,
        grid=(n_groups,),
        in_specs=[smem_block(lambda g: (g, 0, 0)),
                  smem_block(lambda g: (jnp.minimum(g + 1, n_groups - 1), 0, 0)),
                  pl.BlockSpec((PEER_GROUP, D_MODEL), lambda g: (g, 0)),
                  pl.BlockSpec((PEER_GROUP, N_SEL), lambda g: (g, 0)),
                  pl.BlockSpec(memory_space=pl.ANY)],
        out_specs=pl.BlockSpec((PEER_GROUP, D_MODEL), lambda g: (g, 0)),
        out_shape=jax.ShapeDtypeStruct((n, D_MODEL), jnp.float32),
        scratch_shapes=[pltpu.VMEM((PEER_RING, PEER_GROUP, N_SEL, D_MODEL), jnp.uint32),
                        pltpu.VMEM((PEER_GROUP, N_SEL, LANES), jnp.float32),
                        pltpu.VMEM((PEER_GROUP, N_SEL, LANES), jnp.float32),
                        pltpu.VMEM((SUBLANES, D_MODEL), jnp.float32),
                        pltpu.SemaphoreType.DMA((PEER_RING, PEER_GROUP))],
        compiler_params=pltpu.CompilerParams(dimension_semantics=("arbitrary",),
                                             vmem_limit_bytes=VMEM_LIMIT_BYTES),
        name="peer_experts",
    )(eid3, eid3, x, gate, w_packed)


ATT_BLOCK = 256
FAR_BUCKET_MIN_DIST = MAX_DISTANCE


def _attn_kernel(lam_ref, bfar_ref, q_ref, k_ref, v_ref, bd_ref, bs_ref, g_ref, o_ref, m_ref, l_ref, acc_ref,
                 *, out_scale):
    h = pl.program_id(1)
    i = pl.program_id(2)
    nt = (((1,), (1,)), ((), ()))
    m_ref[...] = jnp.full(m_ref.shape, -jnp.inf, jnp.float32)
    l_ref[...] = jnp.zeros(l_ref.shape, jnp.float32)
    acc_ref[...] = jnp.zeros(acc_ref.shape, jnp.float32)

    def step(start, bias):
        kb = k_ref[pl.ds(start, ATT_BLOCK), :]
        vb = v_ref[pl.ds(start, ATT_BLOCK), :]
        for c in range(2):
            s = lax.dot_general(q_ref[:, c * DH:(c + 1) * DH], kb[:, c * DH:(c + 1) * DH], nt,
                                preferred_element_type=jnp.float32) * ATTN_SCALE + bias
            m_old = m_ref[c]
            m_new = jnp.maximum(m_old, jnp.max(s, axis=1, keepdims=True))
            alpha = jnp.exp(m_old - m_new)
            p = jnp.exp(s - m_new[:, 0:1])
            l_ref[c] = alpha * l_ref[c] + jnp.sum(p, axis=1, keepdims=True)
            acc_ref[c] = acc_ref[c] * alpha[:, 0:1] + jnp.dot(p.astype(jnp.bfloat16), vb,
                                                             preferred_element_type=jnp.float32)
            m_ref[c] = m_new

    def far_body(j, carry):
        step(pl.multiple_of(j * ATT_BLOCK, ATT_BLOCK), bfar_ref[h])
        return carry

    lax.fori_loop(0, jnp.maximum(i - 1, 0), far_body, 0)

    @pl.when(i >= 1)
    def _():
        step(pl.multiple_of((i - 1) * ATT_BLOCK, ATT_BLOCK), bs_ref[0])

    step(pl.multiple_of(i * ATT_BLOCK, ATT_BLOCK), bd_ref[0])

    o = acc_ref[0] / l_ref[0][:, 0:1] - lam_ref[0] * (acc_ref[1] / l_ref[1][:, 0:1])
    y = o * lax.rsqrt(jnp.mean(o * o, axis=1, keepdims=True) + RMS_EPS)
    o_ref[...] = y * g_ref[...] * out_scale


def prompt_attention(q, k, v, lam, rel_bias, subln_gain, out_scale, seq):
    n, width = q.shape
    n_heads = width // (2 * DH)
    assert ATT_BLOCK >= FAR_BUCKET_MIN_DIST and seq % ATT_BLOCK == 0
    r = jnp.arange(ATT_BLOCK, dtype=jnp.int32)
    d_diag = r[:, None] - r[None, :]
    bias_diag = jnp.where(d_diag >= 0, jnp.moveaxis(rel_bias[rel_bucket(d_diag)], -1, 0), -jnp.inf)
    bias_sub = jnp.moveaxis(rel_bias[rel_bucket(d_diag + ATT_BLOCK)], -1, 0)
    bias_far = rel_bias[N_BUCKETS - 1]
    nq = seq // ATT_BLOCK
    smem = pl.BlockSpec(memory_space=pltpu.SMEM)
    return pl.pallas_call(
        functools.partial(_attn_kernel, out_scale=out_scale),
        grid=(n // seq, n_heads, nq),
        in_specs=[smem, smem,
                  pl.BlockSpec((ATT_BLOCK, 2 * DH), lambda b, h, i: (b * nq + i, h)),
                  pl.BlockSpec((seq, 2 * DH), lambda b, h, i: (b, h)),
                  pl.BlockSpec((seq, 2 * DH), lambda b, h, i: (b, h)),
                  pl.BlockSpec((1, ATT_BLOCK, ATT_BLOCK), lambda b, h, i: (h, 0, 0)),
                  pl.BlockSpec((1, ATT_BLOCK, ATT_BLOCK), lambda b, h, i: (h, 0, 0)),
                  pl.BlockSpec((1, 2 * DH), lambda b, h, i: (0, 0))],
        out_specs=pl.BlockSpec((ATT_BLOCK, 2 * DH), lambda b, h, i: (b * nq + i, h)),
        out_shape=jax.ShapeDtypeStruct((n, width), jnp.float32),
        scratch_shapes=[pltpu.VMEM((2, ATT_BLOCK, LANES), jnp.float32),
                        pltpu.VMEM((2, ATT_BLOCK, LANES), jnp.float32),
                        pltpu.VMEM((2, ATT_BLOCK, 2 * DH), jnp.float32)],
        compiler_params=pltpu.CompilerParams(dimension_semantics=("parallel", "parallel", "parallel"),
                                             vmem_limit_bytes=VMEM_LIMIT_BYTES),
        name="prompt_attention",
    )(lam.reshape(1), bias_far, q, k, v, bias_diag, bias_sub, subln_gain.reshape(1, 2 * DH))


def rmsnorm(x, g):
    y = x * lax.rsqrt(jnp.mean(x * x, axis=-1, keepdims=True) + RMS_EPS)
    return y * g


def layernorm(x, g, b):
    mu = jnp.mean(x, axis=-1, keepdims=True)
    xc = x - mu
    var = jnp.mean(xc * xc, axis=-1, keepdims=True)
    return xc * lax.rsqrt(var + LN_EPS) * g + b


def rel_bucket(dist):
    n = jnp.maximum(dist, 0)
    max_exact = N_BUCKETS // 2
    nf = jnp.maximum(n, 1).astype(jnp.float32)
    large = max_exact + (jnp.log(nf / max_exact) / math.log(MAX_DISTANCE / max_exact)
                         * (N_BUCKETS - max_exact)).astype(jnp.int32)
    large = jnp.minimum(large, N_BUCKETS - 1)
    return jnp.where(n < max_exact, n, large)


def diff_attend(q, k, v, q_pos, k_pos, lam, rel_bias):
    dist = q_pos[:, None] - k_pos[None, :]
    bias = jnp.moveaxis(rel_bias[rel_bucket(dist)], -1, 0).astype(jnp.float32)
    s = jnp.einsum('bqhcd,bkhcd->bhcqk', q, k).astype(jnp.float32) * ATTN_SCALE + bias[None, :, None]
    s = jnp.where(dist >= 0, s, -jnp.inf)
    p = jax.nn.softmax(s, axis=-1)
    w = p[:, :, 0] - lam * p[:, :, 1]
    return jnp.einsum('bhqk,bkhe->bqhe', w.astype(v.dtype), v)


def chunk_mix(g, w_s, b_s):
    B, T, G, C = g.shape
    pad = (-T) % CHUNK
    gc = jnp.pad(g, ((0, 0), (0, pad), (0, 0), (0, 0))).reshape(B, (T + pad) // CHUNK, CHUNK, G, C)
    w = jnp.where(jnp.tril(jnp.ones((CHUNK, CHUNK), dtype=bool)), w_s, 0.0)
    m = jnp.einsum('gts,bnsgc->bntgc', w.astype(g.dtype), gc) + b_s.T[None, None, :, :, None].astype(g.dtype)
    return m.reshape(B, T + pad, G, C)[:, :T]


def even_mixer(xm, i, layer_idx, P, past_k, past_v):
    B, T, _ = xm.shape
    f32 = jnp.float32
    proj = mm3(xm, P['w_in'], layer=i)
    q, k, v, u, g = jnp.split(proj, [D_ATT, 2 * D_ATT, 3 * D_ATT, 3 * D_ATT + D_SGU], axis=-1)
    q = rmsnorm(q.reshape(B, T, H_A, 2, DH), P['q_gain'][i])
    k = rmsnorm(k.reshape(B, T, H_A, 2, DH), P['k_gain'][i])
    v = v.reshape(B, T, H_A, 2 * DH)
    lam_init = 0.8 - 0.6 * math.exp(-0.3 * layer_idx)
    lam = (jnp.exp(jnp.sum(P['lam_q1'][i].astype(f32) * P['lam_k1'][i].astype(f32)))
           - jnp.exp(jnp.sum(P['lam_q2'][i].astype(f32) * P['lam_k2'][i].astype(f32))) + lam_init)
    if past_k is None:
        bf16 = jnp.bfloat16
        attn = prompt_attention(q.reshape(B * T, D_ATT).astype(bf16), k.reshape(B * T, D_ATT).astype(bf16),
                                v.reshape(B * T, D_ATT).astype(bf16), lam, P['rel_bias'], P['subln_gain'][i],
                                1.0 - lam_init, T).reshape(B, T, D_ATT)
    else:
        past = past_k.shape[1]
        k_all = jnp.concatenate([past_k.astype(k.dtype), k], axis=1)
        v_all = jnp.concatenate([past_v.astype(v.dtype), v], axis=1)
        o = diff_attend(q, k_all, v_all, past + jnp.arange(T, dtype=jnp.int32),
                        jnp.arange(past + T, dtype=jnp.int32), lam, P['rel_bias'])
        attn = (rmsnorm(o, P['subln_gain'][i]) * (1.0 - lam_init)).reshape(B, T, D_ATT)
    u = jax.nn.gelu(u, approximate=False)
    g = layernorm(jax.nn.gelu(g, approximate=False), P['sgu_ln_gain'][i], P['sgu_ln_bias'][i])
    s_out = u * chunk_mix(g.reshape(B, T, SGU_GROUPS, SGU_CG), P['sgu_w'][i], P['sgu_b'][i]).reshape(B, T, D_SGU)
    y = mm3(jnp.concatenate([attn, s_out], axis=-1), P['w_out'], layer=i)
    return y, k.reshape(B, T, H_A, 2 * DH), v, g


def conv_mixer(xm, i, P, buf):
    a = mm3(xm, P['w_pw1'], P['b_pw1'][i], layer=i)
    val, gate = jnp.split(a, 2, axis=-1)
    glu = val * jax.nn.sigmoid(gate)
    full = jnp.concatenate([buf.astype(glu.dtype), glu], axis=1)
    h = lax.conv_general_dilated(full, P['w_dw'][i][:, None, :].astype(full.dtype), (1,), 'VALID',
                                 dimension_numbers=('NWC', 'WIO', 'NWC'),
                                 feature_group_count=D_CONV) + P['b_dw'][i]
    h = jax.nn.silu(layernorm(h, P['conv_ln_gain'][i], P['conv_ln_bias'][i]))
    return mm3(h, P['w_pw2'], P['b_pw2'][i], layer=i), full[:, -(CONV_W - 1):]


def peer(xm, l, P):
    B, T, D = xm.shape
    n = B * T
    xf = xm.reshape(n, D)
    eid, gate = peer_route(mm(xf, P['w_query'], layer=l), P['sub_keys'][l])
    return peer_experts(xf, eid, gate, P['experts'][l]).reshape(B, T, D)


def trunk(x, mods, P, cache_k, cache_v, conv_state, page_table):
    B = x.shape[0]
    sample = page_table is not None
    new_k, new_v, new_g, new_conv = [], [], [], []
    for l in range(DEPTH):
        sh1, sc1, gt1, sh2, sc2, gt2 = jnp.split(mods[l][:, None, :], 6, axis=-1)
        xm = rmsnorm(x, P['norm_gain'][l, 0]) * (1 + sc1) + sh1
        i = l // 2
        if l % 2 == 0:
            if sample:
                past = page_table.shape[1] * PAGE_SIZE
                pk = cache_k[i][page_table].reshape(B, past, H_A, 2, DH)
                pv = cache_v[i][page_table].reshape(B, past, H_A, 2 * DH)
            else:
                pk, pv = None, None
            mix, kr, vr, gr = even_mixer(xm, i, l, P, pk, pv)
            new_k.append(kr)
            new_v.append(vr)
            if sample:
                new_g.append(gr)
        else:
            buf = conv_state[i] if sample else jnp.zeros((B, CONV_W - 1, D_CONV), x.dtype)
            mix, cb = conv_mixer(xm, i, P, buf)
            new_conv.append(cb)
        x = x + gt1 * mix
        hm = rmsnorm(x, P['norm_gain'][l, 1]) * (1 + sc2) + sh2
        x = x + gt2 * peer(hm, l, P)
    g_rows = jnp.stack(new_g) if sample else None
    return x, jnp.stack(new_k), jnp.stack(new_v), g_rows, jnp.stack(new_conv)


def kernel(x_prompt, x_sample, c_prompt, c_sample, cache_k, cache_v, state_conv, page_table, norm_gain, w_ada, b_ada, rel_bias, w_in, q_gain, k_gain, lam_q1, lam_k1, lam_q2, lam_k2, subln_gain, sgu_ln_gain, sgu_ln_bias, sgu_w, sgu_b, w_out, w_pw1, b_pw1, w_dw, b_dw, conv_ln_gain, conv_ln_bias, w_pw2, b_pw2, w_query, sub_keys, expert_u, expert_v):
    bf16 = jnp.bfloat16
    P = dict(norm_gain=norm_gain, rel_bias=rel_bias, w_in=w_in.astype(bf16),
             q_gain=q_gain, k_gain=k_gain, lam_q1=lam_q1, lam_k1=lam_k1, lam_q2=lam_q2, lam_k2=lam_k2,
             subln_gain=subln_gain, sgu_ln_gain=sgu_ln_gain, sgu_ln_bias=sgu_ln_bias, sgu_w=sgu_w,
             sgu_b=sgu_b, w_out=w_out.astype(bf16), w_pw1=w_pw1.astype(bf16), b_pw1=b_pw1, w_dw=w_dw, b_dw=b_dw,
             conv_ln_gain=conv_ln_gain, conv_ln_bias=conv_ln_bias, w_pw2=w_pw2.astype(bf16), b_pw2=b_pw2,
             w_query=w_query.astype(bf16), sub_keys=sub_keys,
             experts=[pack_experts(expert_u, expert_v, layer=l) for l in range(DEPTH)])
    n_prompt = c_prompt.shape[0]
    cs = jax.nn.silu(jnp.concatenate([c_prompt, c_sample], axis=0))
    mods = [mm(cs, w_ada, b_ada[l], layer=l) for l in range(DEPTH)]
    y_prompt, k_prompt, v_prompt, _, conv_prompt = trunk(
        x_prompt, [m[:n_prompt] for m in mods], P, None, None, None, None)
    y_sample, k_sample, v_sample, sgu_v_sample, conv_sample = trunk(
        x_sample, [m[n_prompt:] for m in mods], P, cache_k, cache_v, state_conv, page_table)
    return (y_prompt, y_sample, k_prompt, v_prompt, k_sample, v_sample, sgu_v_sample, conv_prompt, conv_sample)
```
